```python
import jax, jax.numpy as jnp
from jax import lax
import numpy as np

D_MODEL = 1024
BATCH = 2
SEQ = 16384
DEPTH = 1
DEC_BATCH = 1
DEC_SEQ = 16384
PAST_LEN = 128

MIX_WIDTH = D_MODEL
HEAD_DIM = 64
ATTN_WIDTH = MIX_WIDTH // 2
N_HEADS = ATTN_WIDTH // HEAD_DIM
N_KV_HEADS = N_HEADS // 4
GQA_GROUP = N_HEADS // N_KV_HEADS
KV_WIDTH = N_KV_HEADS * HEAD_DIM
FOURIER_WIDTH = MIX_WIDTH - ATTN_WIDTH
FOURIER_GROUP_DIM = 64
N_FOURIER_GROUPS = FOURIER_WIDTH // FOURIER_GROUP_DIM
IN_WIDTH = ATTN_WIDTH + 2 * KV_WIDTH + FOURIER_WIDTH
WINDOW = 128
BLOCK = 128
ROPE_THETA = 500000.0
ROT_DIM = HEAD_DIM // 4
D_FF = 2816
CONV_WIDTH = 3
PLE_DIM = 256
EPS = 1e-6

kernel_name = "hymba_swa_fnet_convffn_encoder"


def rmsnorm(x, g):
    xf = x.astype(jnp.float32)
    y = xf * lax.rsqrt(jnp.mean(xf * xf, axis=-1, keepdims=True) + EPS) * g.astype(jnp.float32)
    return y.astype(x.dtype)


def partial_rope(x):
    S = x.shape[1]
    half = ROT_DIM // 2
    inv_freq = ROPE_THETA ** (-jnp.arange(0, ROT_DIM, 2, dtype=jnp.float32) / ROT_DIM)
    ang = jnp.arange(S, dtype=jnp.float32)[:, None] * inv_freq[None, :]
    cos = jnp.cos(ang)[None, :, None, :]
    sin = jnp.sin(ang)[None, :, None, :]
    xr = x[..., :ROT_DIM].astype(jnp.float32)
    x1, x2 = xr[..., :half], xr[..., half:]
    rot = jnp.concatenate([x1 * cos - x2 * sin, x2 * cos + x1 * sin], axis=-1)
    return jnp.concatenate([rot.astype(x.dtype), x[..., ROT_DIM:]], axis=-1)


def window_attention(q, k, v, sink):
    B, S = q.shape[0], q.shape[1]
    nb = S // BLOCK
    qb = q.reshape(B, nb, BLOCK, N_KV_HEADS, GQA_GROUP, HEAD_DIM)
    pad = ((0, 0), (BLOCK, BLOCK), (0, 0), (0, 0))

    def neighbours(t):
        tp = jnp.pad(t, pad).reshape(B, nb + 2, BLOCK, N_KV_HEADS, HEAD_DIM)
        return jnp.concatenate([tp[:, :-2], tp[:, 1:-1], tp[:, 2:]], axis=2)

    kw = neighbours(k)
    vw = neighbours(v)
    scale = HEAD_DIM ** -0.5
    s = jnp.einsum('bnqhgd,bnkhd->bnhgqk', qb, kw, preferred_element_type=jnp.float32) * scale
    blk = jnp.arange(nb)[:, None, None]
    qpos = blk * BLOCK + jnp.arange(BLOCK)[None, :, None]
    kpos = (blk - 1) * BLOCK + jnp.arange(3 * BLOCK)[None, None, :]
    mask = (jnp.abs(qpos - kpos) <= WINDOW) & (kpos >= 0) & (kpos < S)
    s = jnp.where(mask[None, :, None, None], s, -jnp.inf)
    sk = sink.astype(jnp.float32).reshape(1, 1, N_KV_HEADS, GQA_GROUP, 1, 1)
    m = jnp.maximum(jnp.max(s, axis=-1, keepdims=True), sk)
    p = jnp.exp(s - m)
    denom = jnp.sum(p, axis=-1, keepdims=True) + jnp.exp(sk - m)
    p = (p / denom).astype(v.dtype)
    o = jnp.einsum('bnhgqk,bnkhd->bnqhgd', p, vw, preferred_element_type=jnp.float32)
    return o.astype(q.dtype).reshape(B, S, ATTN_WIDTH)


def fourier_mix(u, w_fourier):
    B, S = u.shape[0], u.shape[1]
    ug = u.reshape(B, S, N_FOURIER_GROUPS, FOURIER_GROUP_DIM).astype(jnp.float32)
    f = jnp.fft.fftn(ug, axes=(1, 3), norm='ortho').real.astype(u.dtype)
    y = jnp.einsum('bsgc,gce->bsge', f, w_fourier)
    return y.reshape(B, S, FOURIER_WIDTH)


def encoder_layer(x, p, attn_norm, w_in, q_norm, k_norm, sink, w_fourier, attn_out_norm,
                  fourier_out_norm, w_out, ffn_norm, w_up, conv_w, conv_b, w_down,
                  w_ple, ple_norm, w_ple_gate, b_ple_gate):
    B, S, _ = x.shape
    h = rmsnorm(x, attn_norm)
    z = h @ w_in
    q, k, v, u = jnp.split(z, [ATTN_WIDTH, ATTN_WIDTH + KV_WIDTH, ATTN_WIDTH + 2 * KV_WIDTH], axis=-1)
    q = partial_rope(rmsnorm(q.reshape(B, S, N_HEADS, HEAD_DIM), q_norm))
    k = partial_rope(rmsnorm(k.reshape(B, S, N_KV_HEADS, HEAD_DIM), k_norm))
    v = v.reshape(B, S, N_KV_HEADS, HEAD_DIM)
    a = window_attention(q, k, v, sink)
    f = fourier_mix(u, w_fourier)
    mixed = jnp.concatenate([rmsnorm(a, attn_out_norm), rmsnorm(f, fourier_out_norm)], axis=-1)
    x = x + mixed @ w_out
    h = rmsnorm(x, ffn_norm)
    hu = h @ w_up
    hp = jnp.pad(hu, ((0, 0), (1, 1), (0, 0)))
    hc = hp[:, :-2] * conv_w[0] + hp[:, 1:-1] * conv_w[1] + hp[:, 2:] * conv_w[2] + conv_b
    g, up = jnp.split(hc, 2, axis=-1)
    x = x + (jax.nn.silu(g) * up) @ w_down
    e = rmsnorm(p @ w_ple, ple_norm)
    gate = jax.nn.sigmoid(x @ w_ple_gate + b_ple_gate)
    return x + gate * e


def setup_inputs(seed: int = 0) -> dict:
    key = jax.random.key(seed)
    ks = jax.random.split(key, 24)
    f32 = jnp.float32

    def nrm(k, shape, scale):
        return jax.random.normal(k, shape, f32) * scale

    def gain(k, shape):
        return 1.0 + 0.02 * jax.random.normal(k, shape, f32)

    return {
        "x_prompt": nrm(ks[0], (BATCH, SEQ, D_MODEL), 1.0),
        "x_sample": nrm(ks[1], (DEC_BATCH, DEC_SEQ, D_MODEL), 1.0),
        "p_prompt": nrm(ks[2], (DEPTH, BATCH, SEQ, PLE_DIM), 1.0),
        "p_sample": nrm(ks[3], (DEPTH, DEC_BATCH, DEC_SEQ, PLE_DIM), 1.0),
        "attn_norm": gain(ks[4], (DEPTH, D_MODEL)),
        "w_in": nrm(ks[5], (DEPTH, D_MODEL, IN_WIDTH), D_MODEL ** -0.5),
        "q_norm": gain(ks[6], (DEPTH, HEAD_DIM)),
        "k_norm": gain(ks[7], (DEPTH, HEAD_DIM)),
        "sink": nrm(ks[8], (DEPTH, N_HEADS), 1.0),
        "w_fourier": nrm(ks[9], (DEPTH, N_FOURIER_GROUPS, FOURIER_GROUP_DIM, FOURIER_GROUP_DIM), FOURIER_GROUP_DIM ** -0.5),
        "attn_out_norm": gain(ks[10], (DEPTH, ATTN_WIDTH)),
        "fourier_out_norm": gain(ks[11], (DEPTH, FOURIER_WIDTH)),
        "w_out": nrm(ks[12], (DEPTH, MIX_WIDTH, D_MODEL), MIX_WIDTH ** -0.5),
        "ffn_norm": gain(ks[13], (DEPTH, D_MODEL)),
        "w_up": nrm(ks[14], (DEPTH, D_MODEL, 2 * D_FF), D_MODEL ** -0.5),
        "conv_w": nrm(ks[15], (DEPTH, CONV_WIDTH, 2 * D_FF), CONV_WIDTH ** -0.5),
        "conv_b": nrm(ks[16], (DEPTH, 2 * D_FF), 0.01),
        "w_down": nrm(ks[17], (DEPTH, D_FF, D_MODEL), D_FF ** -0.5),
        "w_ple": nrm(ks[18], (DEPTH, PLE_DIM, D_MODEL), PLE_DIM ** -0.5),
        "ple_norm": gain(ks[19], (DEPTH, D_MODEL)),
        "w_ple_gate": nrm(ks[20], (DEPTH, D_MODEL, D_MODEL), D_MODEL ** -0.5),
        "b_ple_gate": nrm(ks[21], (DEPTH, D_MODEL), 0.01),
    }


def reference(x_prompt, x_sample, p_prompt, p_sample, attn_norm, w_in, q_norm, k_norm, sink,
              w_fourier, attn_out_norm, fourier_out_norm, w_out, ffn_norm, w_up, conv_w, conv_b,
              w_down, w_ple, ple_norm, w_ple_gate, b_ple_gate):
    y_prompt = x_prompt
    y_sample = x_sample
    for i in range(DEPTH):
        params = (attn_norm[i], w_in[i], q_norm[i], k_norm[i], sink[i], w_fourier[i],
                  attn_out_norm[i], fourier_out_norm[i], w_out[i], ffn_norm[i], w_up[i],
                  conv_w[i], conv_b[i], w_down[i], w_ple[i], ple_norm[i], w_ple_gate[i],
                  b_ple_gate[i])
        y_prompt = encoder_layer(y_prompt, p_prompt[i], *params)
        y_sample = encoder_layer(y_sample, p_sample[i], *params)
    return (y_prompt, y_sample)
```

```python
import functools

import numpy as np
import jax
import jax.numpy as jnp
from jax import lax
from jax.experimental import pallas as pl
from jax.experimental.pallas import tpu as pltpu

F32 = jnp.float32
BF16 = jnp.bfloat16

D_MODEL = 1024
HEAD_DIM = 64
N_HEADS = 8
N_KV_HEADS = 2
GQA_GROUP = N_HEADS // N_KV_HEADS
ATTN_WIDTH = N_HEADS * HEAD_DIM
KV_WIDTH = N_KV_HEADS * HEAD_DIM
FOURIER_WIDTH = 512
FOURIER_GROUP_DIM = 64
N_FOURIER_GROUPS = FOURIER_WIDTH // FOURIER_GROUP_DIM
IN_WIDTH = ATTN_WIDTH + 2 * KV_WIDTH + FOURIER_WIDTH
BLOCK = 128
ROPE_THETA = 500000.0
ROT_DIM = HEAD_DIM // 4
ROT_HALF = ROT_DIM // 2
D_FF = 2816
PLE_DIM = 256
EPS = 1e-6

LANES = 128
BF16_ROWS = 16
VMEM_LIMIT_BYTES = 56 * 1024 * 1024

DFT_N = 128

IN_TILE = 512
ATTN_TILE = 512
S1_COLS = 4096
S2_K1 = 8
FFN_TILE = 512
FFN_HALO = BF16_ROWS
FFN_CHUNK = 256
N_FFN_CHUNKS = D_FF // FFN_CHUNK


def _rms(x, gain, n):
    ms = jnp.sum(x * x, axis=-1, keepdims=True) * (1.0 / n)
    return x * lax.rsqrt(ms + EPS) * gain


def _const_spec(shape):
    zeros = (0,) * len(shape)
    return pl.BlockSpec(shape, lambda *_: zeros, pipeline_mode=pl.Buffered(1))


def _inproj_kernel(x_ref, g_ref, win_ref, qg_ref, kg_ref, cos_ref, sina_ref, sinb_ref, hsum_ref, cs_ref,
                   q_ref, k2_ref, v2_ref, a_ref, b_ref):
    x = x_ref[0]
    h = _rms(x, g_ref[...], D_MODEL).astype(BF16)
    z = jnp.dot(h, win_ref[...], preferred_element_type=F32)
    cos_t = cos_ref[...]
    sin_a = sina_ref[...]
    sin_b = sinb_ref[...]
    hsum = hsum_ref[...]

    def norm_rope(t, gain):
        ssq = jnp.dot((t * t).astype(BF16), hsum, preferred_element_type=F32)
        tn = t * lax.rsqrt(ssq * (1.0 / HEAD_DIM) + EPS) * gain
        up = pltpu.roll(tn, LANES - ROT_HALF, 1)
        dn = pltpu.roll(tn, ROT_HALF, 1)
        return tn * cos_t + up * sin_a + dn * sin_b

    scale = HEAD_DIM ** -0.5
    for t in range(ATTN_WIDTH // LANES):
        qt = norm_rope(z[:, t * LANES:(t + 1) * LANES], qg_ref[...])
        q_ref[0, :, t * LANES:(t + 1) * LANES] = (qt * scale).astype(BF16)

    lane = lax.broadcasted_iota(jnp.int32, (x.shape[0], LANES), 1)
    low = lane < HEAD_DIM

    def dup_heads(t):
        r = pltpu.roll(t, HEAD_DIM, 1)
        return jnp.concatenate([jnp.where(low, t, r), jnp.where(low, r, t)], axis=1)

    k = norm_rope(z[:, ATTN_WIDTH:ATTN_WIDTH + KV_WIDTH], kg_ref[...])
    k2_ref[0] = dup_heads(k).astype(BF16)
    v = z[:, ATTN_WIDTH + KV_WIDTH:ATTN_WIDTH + 2 * KV_WIDTH]
    v2_ref[0] = dup_heads(v).astype(BF16)

    u = z[:, ATTN_WIDTH + 2 * KV_WIDTH:].astype(BF16)
    ab = jnp.dot(u, cs_ref[...], preferred_element_type=F32)
    a_ref[0] = ab[:, :FOURIER_WIDTH].astype(BF16)
    b_ref[0] = ab[:, FOURIER_WIDTH:].astype(BF16)


def _inproj(x, attn_norm, w_in, q_gain, k_gain, cos_t, sin_a, sin_b, hsum, cs_bd):
    nb, s, _ = x.shape
    t = IN_TILE
    tok = lambda w: pl.BlockSpec((1, t, w), lambda b, i: (b, i, 0))
    pos = pl.BlockSpec((t, LANES), lambda b, i: (i, 0))
    out_shape = (
        jax.ShapeDtypeStruct((nb, s, ATTN_WIDTH), BF16),
        jax.ShapeDtypeStruct((nb, s, 2 * KV_WIDTH), BF16),
        jax.ShapeDtypeStruct((nb, s, 2 * KV_WIDTH), BF16),
        jax.ShapeDtypeStruct((nb, s, FOURIER_WIDTH), BF16),
        jax.ShapeDtypeStruct((nb, s, FOURIER_WIDTH), BF16),
    )
    return pl.pallas_call(
        _inproj_kernel,
        grid=(nb, s // t),
        in_specs=[tok(D_MODEL), _const_spec((1, D_MODEL)), _const_spec((D_MODEL, IN_WIDTH)),
                  _const_spec((1, LANES)), _const_spec((1, LANES)), pos, pos, pos,
                  _const_spec((LANES, LANES)), _const_spec((FOURIER_WIDTH, 2 * FOURIER_WIDTH))],
        out_specs=(tok(ATTN_WIDTH), tok(2 * KV_WIDTH), tok(2 * KV_WIDTH), tok(FOURIER_WIDTH),
                   tok(FOURIER_WIDTH)),
        out_shape=out_shape,
        compiler_params=pltpu.CompilerParams(dimension_semantics=("parallel", "parallel"),
                                             vmem_limit_bytes=VMEM_LIMIT_BYTES),
        name="inproj",
    )(x, attn_norm, w_in, q_gain, k_gain, cos_t, sin_a, sin_b, hsum, cs_bd)


def _attn_kernel(sink_ref, q_ref, kp_ref, km_ref, kn_ref, vp_ref, vm_ref, vn_ref, g_ref, o_ref, *, seq):
    i = pl.program_id(1)
    kcat = jnp.concatenate([kp_ref[0], km_ref[0], kn_ref[0]], axis=0)
    vcat = jnp.concatenate([vp_ref[0], vm_ref[0], vn_ref[0]], axis=0)
    nkeys = 3 * BLOCK
    row = lax.broadcasted_iota(jnp.int32, (BLOCK, nkeys), 0)
    col = lax.broadcasted_iota(jnp.int32, (BLOCK, nkeys), 1)
    band = (col >= row) & (col <= row + 2 * BLOCK)
    lane = lax.broadcasted_iota(jnp.int32, (BLOCK, LANES), 1)
    low = lane < HEAD_DIM
    gain = g_ref[...]
    for j in range(ATTN_TILE // BLOCK):
        kpos = (i * (ATTN_TILE // BLOCK) + j - 1) * BLOCK + col
        mask = band & (kpos >= 0) & (kpos < seq)
        tiles = []
        for t in range(ATTN_WIDTH // LANES):
            g = t // (GQA_GROUP // 2)
            qt = q_ref[0, j * BLOCK:(j + 1) * BLOCK, t * LANES:(t + 1) * LANES]
            kg = kcat[j * BLOCK:j * BLOCK + nkeys, g * LANES:(g + 1) * LANES]
            vg = vcat[j * BLOCK:j * BLOCK + nkeys, g * LANES:(g + 1) * LANES]
            outs = []
            for half in range(2):
                sel = low if half == 0 else jnp.logical_not(low)
                qm = jnp.where(sel, qt, jnp.zeros_like(qt))
                s = lax.dot_general(qm, kg, (((1,), (1,)), ((), ())), preferred_element_type=F32)
                s = jnp.where(mask, s, -jnp.inf)
                sk = sink_ref[2 * t + half]
                m = jnp.maximum(jnp.max(s, axis=-1, keepdims=True), sk)
                p = jnp.exp(s - m)
                denom = jnp.sum(p, axis=-1, keepdims=True) + jnp.exp(sk - m)
                p = (p * (1.0 / denom)).astype(BF16)
                outs.append(jnp.dot(p, vg, preferred_element_type=F32))
            tiles.append(jnp.where(low, outs[0], outs[1]))
        a = jnp.concatenate(tiles, axis=1)
        o_ref[0, j * BLOCK:(j + 1) * BLOCK, :] = _rms(a, gain, ATTN_WIDTH).astype(BF16)


def _attention(sink, q, k2, v2, out_gain):
    nb, s, _ = q.shape
    t = ATTN_TILE
    r = t // BLOCK
    last = s // BLOCK - 1
    main = lambda w: pl.BlockSpec((1, t, w), lambda b, i, *_: (b, i, 0))
    prev = pl.BlockSpec((1, BLOCK, 2 * KV_WIDTH), lambda b, i, *_: (b, jnp.maximum(i * r - 1, 0), 0))
    nxt = pl.BlockSpec((1, BLOCK, 2 * KV_WIDTH), lambda b, i, *_: (b, jnp.minimum((i + 1) * r, last), 0))
    grid_spec = pltpu.PrefetchScalarGridSpec(
        num_scalar_prefetch=1,
        grid=(nb, s // t),
        in_specs=[main(ATTN_WIDTH), prev, main(2 * KV_WIDTH), nxt, prev, main(2 * KV_WIDTH), nxt,
                  pl.BlockSpec((1, ATTN_WIDTH), lambda b, i, *_: (0, 0))],
        out_specs=main(ATTN_WIDTH),
    )
    return pl.pallas_call(
        functools.partial(_attn_kernel, seq=s),
        grid_spec=grid_spec,
        out_shape=jax.ShapeDtypeStruct((nb, s, ATTN_WIDTH), BF16),
        compiler_params=pltpu.CompilerParams(dimension_semantics=("parallel", "parallel"),
                                             vmem_limit_bytes=VMEM_LIMIT_BYTES),
        name="attention",
    )(sink, q, k2, k2, k2, v2, v2, v2, out_gain)


def _dft1_kernel(m1_ref, a_ref, b_ref, y_ref):
    ab = jnp.concatenate([a_ref[0], b_ref[0]], axis=0)
    y = jnp.dot(m1_ref[...], ab, preferred_element_type=F32)
    y_ref[0] = y.astype(BF16)


def _dft_stage1(m1, a, b):
    nb = a.shape[0]
    cols = DFT_N * FOURIER_WIDTH
    a = a.reshape(nb, DFT_N, cols)
    b = b.reshape(nb, DFT_N, cols)
    spec = pl.BlockSpec((1, DFT_N, S1_COLS), lambda n, c: (n, 0, c))
    return pl.pallas_call(
        _dft1_kernel,
        grid=(nb, cols // S1_COLS),
        in_specs=[_const_spec((2 * DFT_N, 2 * DFT_N)), spec, spec],
        out_specs=pl.BlockSpec((1, 2 * DFT_N, S1_COLS), lambda n, c: (n, 0, c)),
        out_shape=jax.ShapeDtypeStruct((nb, 2 * DFT_N, cols), BF16),
        compiler_params=pltpu.CompilerParams(dimension_semantics=("parallel", "parallel"),
                                             vmem_limit_bytes=VMEM_LIMIT_BYTES),
        name="dft_stage1",
    )(m1, a, b)


def _dft2_kernel(cn_ref, sn_ref, wc_ref, ws_ref, y_ref, wf_ref, g_ref, o_ref):
    cn = cn_ref[...]
    sn = sn_ref[...]
    wf = wf_ref[...]
    gain = g_ref[...]
    for j in range(S2_K1):
        wc = wc_ref[j:j + 1, :]
        ws = ws_ref[j:j + 1, :]
        gc = cn * wc - sn * ws
        gs = sn * wc + cn * ws
        g = jnp.concatenate([gc, gs], axis=1).astype(BF16)
        y = jnp.concatenate([y_ref[0, 0, j], y_ref[0, 1, j]], axis=0)
        xr = jnp.dot(g, y, preferred_element_type=F32)
        f = jnp.dot(xr.astype(BF16), wf, preferred_element_type=F32)
        o_ref[0, :, j * FOURIER_WIDTH:(j + 1) * FOURIER_WIDTH] = _rms(f, gain, FOURIER_WIDTH).astype(BF16)


def _dft_stage2(cn, sn, wc, ws, y, wf_bd, out_gain):
    nb = y.shape[0]
    y = y.reshape(nb, 2, DFT_N, DFT_N, FOURIER_WIDTH)
    tw = pl.BlockSpec((S2_K1, DFT_N), lambda n, k: (k, 0))
    out = pl.pallas_call(
        _dft2_kernel,
        grid=(nb, DFT_N // S2_K1),
        in_specs=[_const_spec((DFT_N, DFT_N)), _const_spec((DFT_N, DFT_N)), tw, tw,
                  pl.BlockSpec((1, 2, S2_K1, DFT_N, FOURIER_WIDTH), lambda n, k: (n, 0, k, 0, 0)),
                  _const_spec((FOURIER_WIDTH, FOURIER_WIDTH)), _const_spec((1, FOURIER_WIDTH))],
        out_specs=pl.BlockSpec((1, DFT_N, S2_K1 * FOURIER_WIDTH), lambda n, k: (n, 0, k)),
        out_shape=jax.ShapeDtypeStruct((nb, DFT_N, DFT_N * FOURIER_WIDTH), BF16),
        compiler_params=pltpu.CompilerParams(dimension_semantics=("parallel", "parallel"),
                                             vmem_limit_bytes=VMEM_LIMIT_BYTES),
        name="dft_stage2",
    )(cn, sn, wc, ws, y, wf_bd, out_gain)
    return out.reshape(nb, DFT_N * DFT_N, FOURIER_WIDTH)


def _ffn_kernel(xm_ref, xp_ref, xn_ref, am_ref, ap_ref, an_ref, fm_ref, fp_ref, fn_ref, p_ref,
                wout_ref, gffn_ref, wup_ref, cw_ref, cb_ref, wdown_ref, wple_ref, gple_ref, wgate_ref,
                bgate_ref, o_ref):
    i = pl.program_id(1)
    nt = pl.num_programs(1)
    t = FFN_TILE
    ext = t + 2 * FFN_HALO
    x_ext = jnp.concatenate([xp_ref[0], xm_ref[0], xn_ref[0]], axis=0)
    a_ext = jnp.concatenate([ap_ref[0], am_ref[0], an_ref[0]], axis=0)
    f_ext = jnp.concatenate([fp_ref[0], fm_ref[0], fn_ref[0]], axis=0)
    mixed = jnp.concatenate([a_ext, f_ext], axis=1)
    x1 = x_ext + jnp.dot(mixed, wout_ref[...], preferred_element_type=F32)
    h = _rms(x1, gffn_ref[...], D_MODEL)
    row = lax.broadcasted_iota(jnp.int32, (ext, 1), 0)
    first_valid = jnp.where(i == 0, FFN_HALO, 0)
    end_valid = jnp.where(i == nt - 1, t + FFN_HALO, ext)
    h = jnp.where((row >= first_valid) & (row < end_valid), h, 0.0).astype(BF16)

    acc = jnp.zeros((t, D_MODEL), F32)
    for j in range(N_FFN_CHUNKS):
        c0 = 2 * FFN_CHUNK * j
        hu = jnp.dot(h, wup_ref[:, c0:c0 + 2 * FFN_CHUNK], preferred_element_type=F32)
        cw = cw_ref[:, c0:c0 + 2 * FFN_CHUNK]
        before = pltpu.roll(hu, 1, 0)[FFN_HALO:FFN_HALO + t]
        after = pltpu.roll(hu, ext - 1, 0)[FFN_HALO:FFN_HALO + t]
        hc = (before * cw[0:1] + hu[FFN_HALO:FFN_HALO + t] * cw[1:2] + after * cw[2:3]
              + cb_ref[:, c0:c0 + 2 * FFN_CHUNK])
        act = (jax.nn.silu(hc[:, :FFN_CHUNK]) * hc[:, FFN_CHUNK:]).astype(BF16)
        acc = acc + jnp.dot(act, wdown_ref[j * FFN_CHUNK:(j + 1) * FFN_CHUNK, :], preferred_element_type=F32)

    x2 = x1[FFN_HALO:FFN_HALO + t] + acc
    e = jnp.dot(p_ref[0].astype(BF16), wple_ref[...], preferred_element_type=F32)
    e = _rms(e, gple_ref[...], D_MODEL)
    gate = jax.nn.sigmoid(jnp.dot(x2.astype(BF16), wgate_ref[...], preferred_element_type=F32) + bgate_ref[...])
    o_ref[0] = x2 + gate * e


def _ffn(x, a, f, p, w_out, ffn_gain, w_up, conv_w, conv_b, w_down, w_ple, ple_gain, w_gate, b_gate):
    nb, s, _ = x.shape
    t = FFN_TILE
    r = t // FFN_HALO
    last = s // FFN_HALO - 1

    def specs(w):
        return [pl.BlockSpec((1, t, w), lambda b, i: (b, i, 0)),
                pl.BlockSpec((1, FFN_HALO, w), lambda b, i: (b, jnp.maximum(i * r - 1, 0), 0)),
                pl.BlockSpec((1, FFN_HALO, w), lambda b, i: (b, jnp.minimum((i + 1) * r, last), 0))]

    in_specs = (specs(D_MODEL) + specs(ATTN_WIDTH) + specs(FOURIER_WIDTH)
                + [pl.BlockSpec((1, t, PLE_DIM), lambda b, i: (b, i, 0)),
                   _const_spec((D_MODEL, D_MODEL)), _const_spec((1, D_MODEL)),
                   _const_spec((D_MODEL, 2 * D_FF)), _const_spec((3, 2 * D_FF)), _const_spec((1, 2 * D_FF)),
                   _const_spec((D_FF, D_MODEL)), _const_spec((PLE_DIM, D_MODEL)), _const_spec((1, D_MODEL)),
                   _const_spec((D_MODEL, D_MODEL)), _const_spec((1, D_MODEL))])
    return pl.pallas_call(
        _ffn_kernel,
        grid=(nb, s // t),
        in_specs=in_specs,
        out_specs=pl.BlockSpec((1, t, D_MODEL), lambda b, i: (b, i, 0)),
        out_shape=jax.ShapeDtypeStruct((nb, s, D_MODEL), F32),
        compiler_params=pltpu.CompilerParams(dimension_semantics=("parallel", "parallel"),
                                             vmem_limit_bytes=VMEM_LIMIT_BYTES),
        name="ffn",
    )(x, x, x, a, a, a, f, f, f, p, w_out, ffn_gain, w_up, conv_w, conv_b, w_down, w_ple, ple_gain,
      w_gate, b_gate)


def _dft_tables():
    n = np.arange(DFT_N)
    ang = 2.0 * np.pi * ((n[:, None] * n[None, :]) % DFT_N) / DFT_N
    cn, sn = np.cos(ang), np.sin(ang)
    m1 = np.block([[cn, -sn], [-sn, -cn]])
    seq = DFT_N * DFT_N
    tw = 2.0 * np.pi * ((n[:, None] * n[None, :]) % seq) / seq
    ortho = 1.0 / np.sqrt(seq * FOURIER_GROUP_DIM)
    c = np.arange(FOURIER_GROUP_DIM)
    ang_c = 2.0 * np.pi * ((c[:, None] * c[None, :]) % FOURIER_GROUP_DIM) / FOURIER_GROUP_DIM
    eye = np.eye(N_FOURIER_GROUPS)
    cs = np.concatenate([np.kron(eye, np.cos(ang_c)), np.kron(eye, np.sin(ang_c))], axis=1)
    hsum = np.kron(np.eye(LANES // HEAD_DIM), np.ones((HEAD_DIM, HEAD_DIM)))
    f32 = lambda a: jnp.asarray(a, dtype=F32)
    return (f32(m1), f32(cn * ortho), f32(sn * ortho), f32(np.cos(tw)), f32(np.sin(tw)), f32(cs), f32(hsum))


def _rope_tables(seq):
    inv_freq = ROPE_THETA ** (-jnp.arange(0, ROT_DIM, 2, dtype=F32) / ROT_DIM)
    ang = jnp.arange(seq, dtype=F32)[:, None] * inv_freq[None, :]
    cos, sin = jnp.cos(ang), jnp.sin(ang)
    rest = HEAD_DIM - ROT_DIM
    cos_h = jnp.concatenate([cos, cos, jnp.ones((seq, rest), F32)], axis=1)
    sina_h = jnp.concatenate([-sin, jnp.zeros((seq, HEAD_DIM - ROT_HALF), F32)], axis=1)
    sinb_h = jnp.concatenate([jnp.zeros((seq, ROT_HALF), F32), sin, jnp.zeros((seq, rest), F32)], axis=1)
    two = lambda a: jnp.concatenate([a] * (LANES // HEAD_DIM), axis=1)
    return two(cos_h), two(sina_h), two(sinb_h)


def _chunk_cols(a):
    lead = a.shape[:-1]
    g = a[..., :D_FF].reshape(lead + (N_FFN_CHUNKS, FFN_CHUNK))
    u = a[..., D_FF:].reshape(lead + (N_FFN_CHUNKS, FFN_CHUNK))
    return jnp.stack([g, u], axis=-2).reshape(lead + (2 * D_FF,))


def _block_diag(w):
    g, c, e = w.shape
    eye = jnp.eye(g, dtype=w.dtype)
    return (eye[:, None, :, None] * w[:, :, None, :]).reshape(g * c, g * e)


def _layer(x, p, prm, tables):
    (attn_norm, w_in, q_norm, k_norm, sink, w_fourier, attn_out_norm, fourier_out_norm, w_out, ffn_norm,
     w_up, conv_w, conv_b, w_down, w_ple, ple_norm, w_ple_gate, b_ple_gate) = prm
    m1, cn, sn, wc, ws, cs_bd, hsum = tables
    seq = x.shape[1]
    assert seq == DFT_N * DFT_N
    cos_t, sin_a, sin_b = _rope_tables(seq)
    row = lambda v: v.reshape(1, -1)
    two = lambda v: jnp.concatenate([v] * (LANES // HEAD_DIM)).reshape(1, LANES)

    q, k2, v2, a_dft, b_dft = _inproj(x, row(attn_norm), w_in.astype(BF16), two(q_norm), two(k_norm),
                                      cos_t, sin_a, sin_b, hsum.astype(BF16), cs_bd.astype(BF16))
    attn = _attention(sink, q, k2, v2, row(attn_out_norm))
    y = _dft_stage1(m1.astype(BF16), a_dft, b_dft)
    four = _dft_stage2(cn, sn, wc, ws, y, _block_diag(w_fourier).astype(BF16), row(fourier_out_norm))
    return _ffn(x, attn, four, p, w_out.astype(BF16), row(ffn_norm), _chunk_cols(w_up).astype(BF16),
                _chunk_cols(conv_w), _chunk_cols(row(conv_b)), w_down.astype(BF16), w_ple.astype(BF16),
                row(ple_norm), w_ple_gate.astype(BF16), row(b_ple_gate))


def kernel(x_prompt, x_sample, p_prompt, p_sample, attn_norm, w_in, q_norm, k_norm, sink, w_fourier,
           attn_out_norm, fourier_out_norm, w_out, ffn_norm, w_up, conv_w, conv_b, w_down, w_ple, ple_norm,
           w_ple_gate, b_ple_gate):
    stacked = (attn_norm, w_in, q_norm, k_norm, sink, w_fourier, attn_out_norm, fourier_out_norm, w_out,
               ffn_norm, w_up, conv_w, conv_b, w_down, w_ple, ple_norm, w_ple_gate, b_ple_gate)
    tables = _dft_tables()
    y_prompt, y_sample = x_prompt, x_sample
    for i in range(attn_norm.shape[0]):
        prm = tuple(w[i] for w in stacked)
        y_prompt = _layer(y_prompt, p_prompt[i], prm, tables)
        y_sample = _layer(y_sample, p_sample[i], prm, tables)
    return (y_prompt, y_sample)
```

```python
import functools

import numpy as np
import jax
import jax.numpy as jnp
from jax import lax
from jax.experimental import pallas as pl
from jax.experimental.pallas import tpu as pltpu

F32 = jnp.float32
BF16 = jnp.bfloat16

D_MODEL = 1024
HEAD_DIM = 64
N_HEADS = 8
N_KV_HEADS = 2
GQA_GROUP = N_HEADS // N_KV_HEADS
ATTN_WIDTH = N_HEADS * HEAD_DIM
KV_WIDTH = N_KV_HEADS * HEAD_DIM
FOURIER_WIDTH = 512
FOURIER_GROUP_DIM = 64
N_FOURIER_GROUPS = FOURIER_WIDTH // FOURIER_GROUP_DIM
IN_WIDTH = ATTN_WIDTH + 2 * KV_WIDTH + FOURIER_WIDTH
BLOCK = 128
ROPE_THETA = 500000.0
ROT_DIM = HEAD_DIM // 4
ROT_HALF = ROT_DIM // 2
D_FF = 2816
PLE_DIM = 256
EPS = 1e-6

LANES = 128
BF16_ROWS = 16
VMEM_LIMIT_BYTES = 56 * 1024 * 1024

DFT_N = 128

IN_TILE = 512
ATTN_TILE = 512
ATTN_LOOKAHEAD = 2
FFN_LOOKAHEAD = 1
S1_COLS = 4096
S2_K1 = 8
FFN_TILE = 512
FFN_HALO = BF16_ROWS
FFN_CHUNK = 256
N_FFN_CHUNKS = D_FF // FFN_CHUNK


_COMPILER_PARAMS = pltpu.CompilerParams(
    dimension_semantics=("parallel", "parallel"),
    vmem_limit_bytes=VMEM_LIMIT_BYTES,
)


def _rms(x, gain, n):
    ms = jnp.sum(x * x, axis=-1, keepdims=True) * (1.0 / n)
    return x * lax.rsqrt(ms + EPS) * gain


def _const_spec(shape):
    zeros = (0,) * len(shape)
    return pl.BlockSpec(shape, lambda *_: zeros, pipeline_mode=pl.Buffered(1))


def _inproj_kernel(x_ref, g_ref, win_ref, qg_ref, kg_ref, cos_ref, sina_ref, sinb_ref, hsum_ref, cs_ref,
                   q_ref, k2_ref, v2_ref, a_ref, b_ref):
    x = x_ref[0]
    h = _rms(x, g_ref[...], D_MODEL).astype(BF16)
    z = jnp.dot(h, win_ref[...], preferred_element_type=F32)
    cos_t = cos_ref[...]
    sin_a = sina_ref[...]
    sin_b = sinb_ref[...]
    hsum = hsum_ref[...]

    def norm_rope(t, gain):
        ssq = jnp.dot((t * t).astype(BF16), hsum, preferred_element_type=F32)
        tn = t * lax.rsqrt(ssq * (1.0 / HEAD_DIM) + EPS) * gain
        up = pltpu.roll(tn, LANES - ROT_HALF, 1)
        dn = pltpu.roll(tn, ROT_HALF, 1)
        return tn * cos_t + up * sin_a + dn * sin_b

    scale = HEAD_DIM ** -0.5
    for t in range(ATTN_WIDTH // LANES):
        qt = norm_rope(z[:, t * LANES:(t + 1) * LANES], qg_ref[...])
        q_ref[0, :, t * LANES:(t + 1) * LANES] = (qt * scale).astype(BF16)

    lane = lax.broadcasted_iota(jnp.int32, (x.shape[0], LANES), 1)
    low = lane < HEAD_DIM

    def dup_heads(t):
        r = pltpu.roll(t, HEAD_DIM, 1)
        return jnp.concatenate([jnp.where(low, t, r), jnp.where(low, r, t)], axis=1)

    k = norm_rope(z[:, ATTN_WIDTH:ATTN_WIDTH + KV_WIDTH], kg_ref[...])
    k2_ref[0] = dup_heads(k).astype(BF16)
    v = z[:, ATTN_WIDTH + KV_WIDTH:ATTN_WIDTH + 2 * KV_WIDTH]
    v2_ref[0] = dup_heads(v).astype(BF16)

    u = z[:, ATTN_WIDTH + 2 * KV_WIDTH:].astype(BF16)
    ab = jnp.dot(u, cs_ref[...], preferred_element_type=F32)
    a_ref[0] = ab[:, :FOURIER_WIDTH].astype(BF16)
    b_ref[0] = ab[:, FOURIER_WIDTH:].astype(BF16)


def _inproj(x, attn_norm, w_in, q_gain, k_gain, cos_t, sin_a, sin_b, hsum, cs_bd):
    nb, s, _ = x.shape
    t = IN_TILE
    tok = lambda w: pl.BlockSpec((1, t, w), lambda b, i: (b, i, 0))
    pos = pl.BlockSpec((t, LANES), lambda b, i: (i, 0))
    out_shape = (
        jax.ShapeDtypeStruct((nb, s, ATTN_WIDTH), BF16),
        jax.ShapeDtypeStruct((nb, s, 2 * KV_WIDTH), BF16),
        jax.ShapeDtypeStruct((nb, s, 2 * KV_WIDTH), BF16),
        jax.ShapeDtypeStruct((nb, s, FOURIER_WIDTH), BF16),
        jax.ShapeDtypeStruct((nb, s, FOURIER_WIDTH), BF16),
    )
    return pl.pallas_call(
        _inproj_kernel,
        grid=(nb, s // t),
        in_specs=[tok(D_MODEL), _const_spec((1, D_MODEL)), _const_spec((D_MODEL, IN_WIDTH)),
                  _const_spec((1, LANES)), _const_spec((1, LANES)), pos, pos, pos,
                  _const_spec((LANES, LANES)), _const_spec((FOURIER_WIDTH, 2 * FOURIER_WIDTH))],
        out_specs=(tok(ATTN_WIDTH), tok(2 * KV_WIDTH), tok(2 * KV_WIDTH), tok(FOURIER_WIDTH),
                   tok(FOURIER_WIDTH)),
        out_shape=out_shape,
        compiler_params=_COMPILER_PARAMS,
        name="inproj",
    )(x, attn_norm, w_in, q_gain, k_gain, cos_t, sin_a, sin_b, hsum, cs_bd)


def _attn_kernel(sink_ref, q_ref, kp_ref, km_ref, kn_ref, vp_ref, vm_ref, vn_ref, g_ref, o_ref, *, seq):
    i = pl.program_id(1)
    kcat = jnp.concatenate([kp_ref[0], km_ref[0], kn_ref[0]], axis=0)
    vcat = jnp.concatenate([vp_ref[0], vm_ref[0], vn_ref[0]], axis=0)
    nkeys = 3 * BLOCK
    row = lax.broadcasted_iota(jnp.int32, (BLOCK, nkeys), 0)
    col = lax.broadcasted_iota(jnp.int32, (BLOCK, nkeys), 1)
    band = (col >= row) & (col <= row + 2 * BLOCK)
    lane = lax.broadcasted_iota(jnp.int32, (BLOCK, LANES), 1)
    low = lane < HEAD_DIM
    gain = g_ref[...]
    units = [(j, g) for j in range(ATTN_TILE // BLOCK) for g in range(N_KV_HEADS)]

    def scores(j, g):
        kg = kcat[j * BLOCK:j * BLOCK + nkeys, g * LANES:(g + 1) * LANES]
        qs = []
        for hh in range(GQA_GROUP):
            t = (g * GQA_GROUP + hh) // 2
            qt = q_ref[0, j * BLOCK:(j + 1) * BLOCK, t * LANES:(t + 1) * LANES]
            sel = low if hh % 2 == 0 else jnp.logical_not(low)
            qs.append(jnp.where(sel, qt, jnp.zeros_like(qt)))
        qg = jnp.concatenate(qs, axis=0)
        return lax.dot_general(qg, kg, (((1,), (1,)), ((), ())), preferred_element_type=F32)

    def softmax_pv(j, g, s):
        kpos = (i * (ATTN_TILE // BLOCK) + j - 1) * BLOCK + col
        mask = band & (kpos >= 0) & (kpos < seq)
        vg = vcat[j * BLOCK:j * BLOCK + nkeys, g * LANES:(g + 1) * LANES]
        sk = jnp.concatenate([jnp.full((BLOCK, 1), sink_ref[g * GQA_GROUP + hh], F32)
                              for hh in range(GQA_GROUP)], axis=0)
        s = jnp.where(mask[None], s.reshape(GQA_GROUP, BLOCK, nkeys), -jnp.inf)
        s = s.reshape(GQA_GROUP * BLOCK, nkeys)
        m = jnp.maximum(jnp.max(s, axis=-1, keepdims=True), sk)
        p = jnp.exp(s - m)
        denom = jnp.sum(p, axis=-1, keepdims=True) + jnp.exp(sk - m)
        p = (p * (1.0 / denom)).astype(BF16)
        o = jnp.dot(p, vg, preferred_element_type=F32)
        return [jnp.where(low, o[(2 * pair) * BLOCK:(2 * pair + 1) * BLOCK],
                          o[(2 * pair + 1) * BLOCK:(2 * pair + 2) * BLOCK])
                for pair in range(GQA_GROUP // 2)]

    pending = [scores(*u) for u in units[:ATTN_LOOKAHEAD]]
    tiles = []
    for n, (j, g) in enumerate(units):
        if n + ATTN_LOOKAHEAD < len(units):
            pending.append(scores(*units[n + ATTN_LOOKAHEAD]))
        tiles += softmax_pv(j, g, pending.pop(0))
        if g == N_KV_HEADS - 1:
            a = jnp.concatenate(tiles, axis=1)
            o_ref[0, j * BLOCK:(j + 1) * BLOCK, :] = _rms(a, gain, ATTN_WIDTH).astype(BF16)
            tiles = []


def _attention(sink, q, k2, v2, out_gain):
    nb, s, _ = q.shape
    t = ATTN_TILE
    r = t // BLOCK
    last = s // BLOCK - 1
    main = lambda w: pl.BlockSpec((1, t, w), lambda b, i, *_: (b, i, 0))
    prev = pl.BlockSpec((1, BLOCK, 2 * KV_WIDTH), lambda b, i, *_: (b, jnp.maximum(i * r - 1, 0), 0))
    nxt = pl.BlockSpec((1, BLOCK, 2 * KV_WIDTH), lambda b, i, *_: (b, jnp.minimum((i + 1) * r, last), 0))
    grid_spec = pltpu.PrefetchScalarGridSpec(
        num_scalar_prefetch=1,
        grid=(nb, s // t),
        in_specs=[main(ATTN_WIDTH), prev, main(2 * KV_WIDTH), nxt, prev, main(2 * KV_WIDTH), nxt,
                  pl.BlockSpec((1, ATTN_WIDTH), lambda b, i, *_: (0, 0))],
        out_specs=main(ATTN_WIDTH),
    )
    return pl.pallas_call(
        functools.partial(_attn_kernel, seq=s),
        grid_spec=grid_spec,
        out_shape=jax.ShapeDtypeStruct((nb, s, ATTN_WIDTH), BF16),
        compiler_params=_COMPILER_PARAMS,
        name="attention",
    )(sink, q, k2, k2, k2, v2, v2, v2, out_gain)


def _dft1_kernel(m1_ref, a_ref, b_ref, y_ref):
    ab = jnp.concatenate([a_ref[0], b_ref[0]], axis=0)
    y = jnp.dot(m1_ref[...], ab, preferred_element_type=F32)
    y_ref[0] = y.astype(BF16)


def _dft_stage1(m1, a, b):
    nb = a.shape[0]
    cols = DFT_N * FOURIER_WIDTH
    a = a.reshape(nb, DFT_N, cols)
    b = b.reshape(nb, DFT_N, cols)
    spec = pl.BlockSpec((1, DFT_N, S1_COLS), lambda n, c: (n, 0, c))
    return pl.pallas_call(
        _dft1_kernel,
        grid=(nb, cols // S1_COLS),
        in_specs=[_const_spec((2 * DFT_N, 2 * DFT_N)), spec, spec],
        out_specs=pl.BlockSpec((1, 2 * DFT_N, S1_COLS), lambda n, c: (n, 0, c)),
        out_shape=jax.ShapeDtypeStruct((nb, 2 * DFT_N, cols), BF16),
        compiler_params=_COMPILER_PARAMS,
        name="dft_stage1",
    )(m1, a, b)


def _dft2_kernel(cn_ref, sn_ref, wc_ref, ws_ref, y_ref, wf_ref, g_ref, o_ref):
    cn = cn_ref[...]
    sn = sn_ref[...]
    wf = wf_ref[...]
    gain = g_ref[...]
    for j in range(S2_K1):
        wc = wc_ref[j:j + 1, :]
        ws = ws_ref[j:j + 1, :]
        gc = cn * wc - sn * ws
        gs = sn * wc + cn * ws
        g = jnp.concatenate([gc, gs], axis=1).astype(BF16)
        y = jnp.concatenate([y_ref[0, 0, j], y_ref[0, 1, j]], axis=0)
        xr = jnp.dot(g, y, preferred_element_type=F32)
        f = jnp.dot(xr.astype(BF16), wf, preferred_element_type=F32)
        o_ref[0, :, j * FOURIER_WIDTH:(j + 1) * FOURIER_WIDTH] = _rms(f, gain, FOURIER_WIDTH).astype(BF16)


def _dft_stage2(cn, sn, wc, ws, y, wf_bd, out_gain):
    nb = y.shape[0]
    y = y.reshape(nb, 2, DFT_N, DFT_N, FOURIER_WIDTH)
    tw = pl.BlockSpec((S2_K1, DFT_N), lambda n, k: (k, 0))
    out = pl.pallas_call(
        _dft2_kernel,
        grid=(nb, DFT_N // S2_K1),
        in_specs=[_const_spec((DFT_N, DFT_N)), _const_spec((DFT_N, DFT_N)), tw, tw,
                  pl.BlockSpec((1, 2, S2_K1, DFT_N, FOURIER_WIDTH), lambda n, k: (n, 0, k, 0, 0)),
                  _const_spec((FOURIER_WIDTH, FOURIER_WIDTH)), _const_spec((1, FOURIER_WIDTH))],
        out_specs=pl.BlockSpec((1, DFT_N, S2_K1 * FOURIER_WIDTH), lambda n, k: (n, 0, k)),
        out_shape=jax.ShapeDtypeStruct((nb, DFT_N, DFT_N * FOURIER_WIDTH), BF16),
        compiler_params=_COMPILER_PARAMS,
        name="dft_stage2",
    )(cn, sn, wc, ws, y, wf_bd, out_gain)
    return out.reshape(nb, DFT_N * DFT_N, FOURIER_WIDTH)


def _ffn_kernel(xm_ref, xp_ref, xn_ref, am_ref, ap_ref, an_ref, fm_ref, fp_ref, fn_ref, p_ref,
                wout_ref, gffn_ref, wup_ref, cw_ref, cb_ref, wdown_ref, wple_ref, gple_ref, wgate_ref,
                bgate_ref, o_ref):
    i = pl.program_id(1)
    nt = pl.num_programs(1)
    t = FFN_TILE
    ext = t + 2 * FFN_HALO
    x_ext = jnp.concatenate([xp_ref[0], xm_ref[0], xn_ref[0]], axis=0)
    a_ext = jnp.concatenate([ap_ref[0], am_ref[0], an_ref[0]], axis=0)
    f_ext = jnp.concatenate([fp_ref[0], fm_ref[0], fn_ref[0]], axis=0)
    mixed = jnp.concatenate([a_ext, f_ext], axis=1)
    x1 = x_ext + jnp.dot(mixed, wout_ref[...], preferred_element_type=F32)
    h = _rms(x1, gffn_ref[...], D_MODEL)
    row = lax.broadcasted_iota(jnp.int32, (ext, 1), 0)
    first_valid = jnp.where(i == 0, FFN_HALO, 0)
    end_valid = jnp.where(i == nt - 1, t + FFN_HALO, ext)
    h = jnp.where((row >= first_valid) & (row < end_valid), h, 0.0).astype(BF16)

    acc = jnp.zeros((t, D_MODEL), F32)
    def up_proj(j):
        return tuple(jnp.dot(h, wup_ref[:, c0:c0 + FFN_CHUNK], preferred_element_type=F32)
                     for c0 in (j * FFN_CHUNK, D_FF + j * FFN_CHUNK))

    def conv(hu, c0):
        cw = cw_ref[:, c0:c0 + FFN_CHUNK]
        before = pltpu.roll(hu, 1, 0)[FFN_HALO:FFN_HALO + t]
        after = pltpu.roll(hu, ext - 1, 0)[FFN_HALO:FFN_HALO + t]
        return (before * cw[0:1] + hu[FFN_HALO:FFN_HALO + t] * cw[1:2] + after * cw[2:3]
                + cb_ref[:, c0:c0 + FFN_CHUNK])

    pending = [up_proj(j) for j in range(FFN_LOOKAHEAD)]
    for j in range(N_FFN_CHUNKS):
        if j + FFN_LOOKAHEAD < N_FFN_CHUNKS:
            pending.append(up_proj(j + FFN_LOOKAHEAD))
        hu_gate, hu_up = pending.pop(0)
        gate = conv(hu_gate, j * FFN_CHUNK)
        up = conv(hu_up, D_FF + j * FFN_CHUNK)
        act = (jax.nn.silu(gate) * up).astype(BF16)
        acc = acc + jnp.dot(act, wdown_ref[j * FFN_CHUNK:(j + 1) * FFN_CHUNK, :], preferred_element_type=F32)

    x2 = x1[FFN_HALO:FFN_HALO + t] + acc
    e = jnp.dot(p_ref[0].astype(BF16), wple_ref[...], preferred_element_type=F32)
    e = _rms(e, gple_ref[...], D_MODEL)
    gate = jax.nn.sigmoid(jnp.dot(x2.astype(BF16), wgate_ref[...], preferred_element_type=F32) + bgate_ref[...])
    o_ref[0] = x2 + gate * e


def _ffn(x, a, f, p, w_out, ffn_gain, w_up, conv_w, conv_b, w_down, w_ple, ple_gain, w_gate, b_gate):
    nb, s, _ = x.shape
    t = FFN_TILE
    r = t // FFN_HALO
    last = s // FFN_HALO - 1

    def specs(w):
        return [pl.BlockSpec((1, t, w), lambda b, i: (b, i, 0)),
                pl.BlockSpec((1, FFN_HALO, w), lambda b, i: (b, jnp.maximum(i * r - 1, 0), 0)),
                pl.BlockSpec((1, FFN_HALO, w), lambda b, i: (b, jnp.minimum((i + 1) * r, last), 0))]

    in_specs = (specs(D_MODEL) + specs(ATTN_WIDTH) + specs(FOURIER_WIDTH)
                + [pl.BlockSpec((1, t, PLE_DIM), lambda b, i: (b, i, 0)),
                   _const_spec((D_MODEL, D_MODEL)), _const_spec((1, D_MODEL)),
                   _const_spec((D_MODEL, 2 * D_FF)), _const_spec((3, 2 * D_FF)), _const_spec((1, 2 * D_FF)),
                   _const_spec((D_FF, D_MODEL)), _const_spec((PLE_DIM, D_MODEL)), _const_spec((1, D_MODEL)),
                   _const_spec((D_MODEL, D_MODEL)), _const_spec((1, D_MODEL))])
    return pl.pallas_call(
        _ffn_kernel,
        grid=(nb, s // t),
        in_specs=in_specs,
        out_specs=pl.BlockSpec((1, t, D_MODEL), lambda b, i: (b, i, 0)),
        out_shape=jax.ShapeDtypeStruct((nb, s, D_MODEL), F32),
        compiler_params=_COMPILER_PARAMS,
        name="ffn",
    )(x, x, x, a, a, a, f, f, f, p, w_out, ffn_gain, w_up, conv_w, conv_b, w_down, w_ple, ple_gain,
      w_gate, b_gate)


def _dft_tables():
    n = np.arange(DFT_N)
    ang = 2.0 * np.pi * ((n[:, None] * n[None, :]) % DFT_N) / DFT_N
    cn, sn = np.cos(ang), np.sin(ang)
    m1 = np.block([[cn, -sn], [-sn, -cn]])
    seq = DFT_N * DFT_N
    tw = 2.0 * np.pi * ((n[:, None] * n[None, :]) % seq) / seq
    ortho = 1.0 / np.sqrt(seq * FOURIER_GROUP_DIM)
    c = np.arange(FOURIER_GROUP_DIM)
    ang_c = 2.0 * np.pi * ((c[:, None] * c[None, :]) % FOURIER_GROUP_DIM) / FOURIER_GROUP_DIM
    eye = np.eye(N_FOURIER_GROUPS)
    cs = np.concatenate([np.kron(eye, np.cos(ang_c)), np.kron(eye, np.sin(ang_c))], axis=1)
    hsum = np.kron(np.eye(LANES // HEAD_DIM), np.ones((HEAD_DIM, HEAD_DIM)))
    f32 = lambda a: jnp.asarray(a, dtype=F32)
    return (f32(m1), f32(cn * ortho), f32(sn * ortho), f32(np.cos(tw)), f32(np.sin(tw)), f32(cs), f32(hsum))


def _rope_tables(seq):
    inv_freq = ROPE_THETA ** (-jnp.arange(0, ROT_DIM, 2, dtype=F32) / ROT_DIM)
    inv_head = jnp.concatenate([inv_freq, inv_freq, jnp.zeros((HEAD_DIM - ROT_DIM,), F32)])
    inv_lane = jnp.concatenate([inv_head] * (LANES // HEAD_DIM))
    ang = jnp.arange(seq, dtype=F32)[:, None] * inv_lane[None, :]
    cos, sin = jnp.cos(ang), jnp.sin(ang)
    d = (jnp.arange(LANES) % HEAD_DIM)[None, :]
    sin_a = jnp.where(d < ROT_HALF, -sin, 0.0)
    sin_b = jnp.where((d >= ROT_HALF) & (d < ROT_DIM), sin, 0.0)
    return cos, sin_a, sin_b


def _block_diag(w):
    g, c, e = w.shape
    eye = jnp.eye(g, dtype=w.dtype)
    return (eye[:, None, :, None] * w[:, :, None, :]).reshape(g * c, g * e)


def _layer(x, p, prm, tables):
    (attn_norm, w_in, q_norm, k_norm, sink, w_fourier, attn_out_norm, fourier_out_norm, w_out, ffn_norm,
     w_up, conv_w, conv_b, w_down, w_ple, ple_norm, w_ple_gate, b_ple_gate) = prm
    m1, cn, sn, wc, ws, cs_bd, hsum = tables
    seq = x.shape[1]
    assert seq == DFT_N * DFT_N
    cos_t, sin_a, sin_b = _rope_tables(seq)
    row = lambda v: v.reshape(1, -1)
    two = lambda v: jnp.concatenate([v] * (LANES // HEAD_DIM)).reshape(1, LANES)

    q, k2, v2, a_dft, b_dft = _inproj(x, row(attn_norm), w_in.astype(BF16), two(q_norm), two(k_norm),
                                      cos_t, sin_a, sin_b, hsum.astype(BF16), cs_bd.astype(BF16))
    attn = _attention(sink, q, k2, v2, row(attn_out_norm))
    y = _dft_stage1(m1.astype(BF16), a_dft, b_dft)
    four = _dft_stage2(cn, sn, wc, ws, y, _block_diag(w_fourier).astype(BF16), row(fourier_out_norm))
    return _ffn(x, attn, four, p, w_out.astype(BF16), row(ffn_norm), w_up.astype(BF16), conv_w, row(conv_b),
                w_down.astype(BF16), w_ple.astype(BF16), row(ple_norm), w_ple_gate.astype(BF16),
                row(b_ple_gate))


def kernel(x_prompt, x_sample, p_prompt, p_sample, attn_norm, w_in, q_norm, k_norm, sink, w_fourier,
           attn_out_norm, fourier_out_norm, w_out, ffn_norm, w_up, conv_w, conv_b, w_down, w_ple, ple_norm,
           w_ple_gate, b_ple_gate):
    stacked = (attn_norm, w_in, q_norm, k_norm, sink, w_fourier, attn_out_norm, fourier_out_norm, w_out,
               ffn_norm, w_up, conv_w, conv_b, w_down, w_ple, ple_norm, w_ple_gate, b_ple_gate)
    tables = _dft_tables()
    y_prompt, y_sample = x_prompt, x_sample
    for i in range(attn_norm.shape[0]):
        prm = tuple(w[i] for w in stacked)
        y_prompt = _layer(y_prompt, p_prompt[i], prm, tables)
        y_sample = _layer(y_sample, p_sample[i], prm, tables)
    return (y_prompt, y_sample)
```

```python
import functools

import numpy as np
import jax
import jax.numpy as jnp
from jax import lax
from jax.experimental import pallas as pl
from jax.experimental.pallas import tpu as pltpu

F32 = jnp.float32
BF16 = jnp.bfloat16

D_MODEL = 1024
HEAD_DIM = 64
N_HEADS = 8
N_KV_HEADS = 2
GQA_GROUP = N_HEADS // N_KV_HEADS
ATTN_WIDTH = N_HEADS * HEAD_DIM
KV_WIDTH = N_KV_HEADS * HEAD_DIM
FOURIER_WIDTH = 512
FOURIER_GROUP_DIM = 64
N_FOURIER_GROUPS = FOURIER_WIDTH // FOURIER_GROUP_DIM
IN_WIDTH = ATTN_WIDTH + 2 * KV_WIDTH + FOURIER_WIDTH
BLOCK = 128
ROPE_THETA = 500000.0
ROT_DIM = HEAD_DIM // 4
ROT_HALF = ROT_DIM // 2
D_FF = 2816
PLE_DIM = 256
EPS = 1e-6
LOG2E = 1.4426950408889634

LANES = 128
BF16_ROWS = 16
VMEM_LIMIT_BYTES = 56 * 1024 * 1024

DFT_N = 128

IN_TILE = 512
ATTN_TILE = 512
ATTN_LOOKAHEAD = 2
FFN_LOOKAHEAD = 2
S1_COLS = 4096
S2_K1 = 8
FFN_TILE = 256
FFN_HALO = BF16_ROWS
FFN_CHUNK = 256
N_FFN_CHUNKS = D_FF // FFN_CHUNK


_COMPILER_PARAMS = pltpu.CompilerParams(
    dimension_semantics=("parallel", "parallel"),
    vmem_limit_bytes=VMEM_LIMIT_BYTES,
)


def _rms(x, gain, n):
    ms = jnp.sum(x * x, axis=-1, keepdims=True) * (1.0 / n)
    return x * lax.rsqrt(ms + EPS) * gain


def _const_spec(shape):
    zeros = (0,) * len(shape)
    return pl.BlockSpec(shape, lambda *_: zeros, pipeline_mode=pl.Buffered(1))


def _inproj_kernel(x_ref, g_ref, win_ref, qg_ref, kg_ref, cos_ref, sina_ref, sinb_ref, hsum_ref, cs_ref,
                   q_ref, k2_ref, v2_ref, a_ref, b_ref):
    x = x_ref[0]
    h = _rms(x, g_ref[...], D_MODEL).astype(BF16)
    z = jnp.dot(h, win_ref[...], preferred_element_type=F32)
    cos_t = cos_ref[...]
    sin_a = sina_ref[...]
    sin_b = sinb_ref[...]
    hsum = hsum_ref[...]

    def norm_rope(t, gain):
        ssq = jnp.dot((t * t).astype(BF16), hsum, preferred_element_type=F32)
        tn = t * lax.rsqrt(ssq * (1.0 / HEAD_DIM) + EPS) * gain
        up = pltpu.roll(tn, LANES - ROT_HALF, 1)
        dn = pltpu.roll(tn, ROT_HALF, 1)
        return tn * cos_t + up * sin_a + dn * sin_b

    scale = HEAD_DIM ** -0.5 * LOG2E
    for t in range(ATTN_WIDTH // LANES):
        qt = norm_rope(z[:, t * LANES:(t + 1) * LANES], qg_ref[...])
        q_ref[0, :, t * LANES:(t + 1) * LANES] = (qt * scale).astype(BF16)

    lane = lax.broadcasted_iota(jnp.int32, (x.shape[0], LANES), 1)
    low = lane < HEAD_DIM

    def dup_heads(t):
        r = pltpu.roll(t, HEAD_DIM, 1)
        return jnp.concatenate([jnp.where(low, t, r), jnp.where(low, r, t)], axis=1)

    k = norm_rope(z[:, ATTN_WIDTH:ATTN_WIDTH + KV_WIDTH], kg_ref[...])
    k2_ref[0] = dup_heads(k).astype(BF16)
    v = z[:, ATTN_WIDTH + KV_WIDTH:ATTN_WIDTH + 2 * KV_WIDTH]
    v2_ref[0] = dup_heads(v).astype(BF16)

    u = z[:, ATTN_WIDTH + 2 * KV_WIDTH:].astype(BF16)
    ab = jnp.dot(u, cs_ref[...], preferred_element_type=F32)
    a_ref[0] = ab[:, :FOURIER_WIDTH].astype(BF16)
    b_ref[0] = ab[:, FOURIER_WIDTH:].astype(BF16)


def _inproj(x, attn_norm, w_in, q_gain, k_gain, cos_t, sin_a, sin_b, hsum, cs_bd):
    nb, s, _ = x.shape
    t = IN_TILE
    tok = lambda w: pl.BlockSpec((1, t, w), lambda b, i: (b, i, 0))
    pos = pl.BlockSpec((t, LANES), lambda b, i: (i, 0))
    out_shape = (
        jax.ShapeDtypeStruct((nb, s, ATTN_WIDTH), BF16),
        jax.ShapeDtypeStruct((nb, s, 2 * KV_WIDTH), BF16),
        jax.ShapeDtypeStruct((nb, s, 2 * KV_WIDTH), BF16),
        jax.ShapeDtypeStruct((nb, s, FOURIER_WIDTH), BF16),
        jax.ShapeDtypeStruct((nb, s, FOURIER_WIDTH), BF16),
    )
    return pl.pallas_call(
        _inproj_kernel,
        grid=(nb, s // t),
        in_specs=[tok(D_MODEL), _const_spec((1, D_MODEL)), _const_spec((D_MODEL, IN_WIDTH)),
                  _const_spec((1, LANES)), _const_spec((1, LANES)), pos, pos, pos,
                  _const_spec((LANES, LANES)), _const_spec((FOURIER_WIDTH, 2 * FOURIER_WIDTH))],
        out_specs=(tok(ATTN_WIDTH), tok(2 * KV_WIDTH), tok(2 * KV_WIDTH), tok(FOURIER_WIDTH),
                   tok(FOURIER_WIDTH)),
        out_shape=out_shape,
        compiler_params=_COMPILER_PARAMS,
        name="inproj",
    )(x, attn_norm, w_in, q_gain, k_gain, cos_t, sin_a, sin_b, hsum, cs_bd)


def _attn_kernel(sink_ref, q_ref, kp_ref, km_ref, kn_ref, vp_ref, vm_ref, vn_ref, g_ref, o_ref, *, seq):
    i = pl.program_id(1)
    kcat = jnp.concatenate([kp_ref[0], km_ref[0], kn_ref[0]], axis=0)
    vcat = jnp.concatenate([vp_ref[0], vm_ref[0], vn_ref[0]], axis=0)
    nkeys = 3 * BLOCK
    diff = (lax.broadcasted_iota(jnp.int32, (BLOCK, BLOCK), 1)
            - lax.broadcasted_iota(jnp.int32, (BLOCK, BLOCK), 0))
    lane = lax.broadcasted_iota(jnp.int32, (BLOCK, LANES), 1)
    low = lane < HEAD_DIM
    gain = g_ref[...]
    units = [(j, g) for j in range(ATTN_TILE // BLOCK) for g in range(N_KV_HEADS)]

    def scores(j, g):
        kg = kcat[j * BLOCK:j * BLOCK + nkeys, g * LANES:(g + 1) * LANES]
        qs = []
        for hh in range(GQA_GROUP):
            t = (g * GQA_GROUP + hh) // 2
            qt = q_ref[0, j * BLOCK:(j + 1) * BLOCK, t * LANES:(t + 1) * LANES]
            sel = low if hh % 2 == 0 else jnp.logical_not(low)
            qs.append(jnp.where(sel, qt, jnp.zeros_like(qt)))
        qg = jnp.concatenate(qs, axis=0)
        return lax.dot_general(qg, kg, (((1,), (1,)), ((), ())), preferred_element_type=F32)

    def softmax_pv(j, g, s):
        qblk = i * (ATTN_TILE // BLOCK) + j
        lo = jnp.where(qblk > 0, 0, BLOCK)
        hi = jnp.where(qblk < seq // BLOCK - 1, 0, -BLOCK)
        mask_prev = diff >= lo
        mask_next = diff <= hi
        vg = vcat[j * BLOCK:j * BLOCK + nkeys, g * LANES:(g + 1) * LANES]
        ps, rs = [], []
        for hh in range(GQA_GROUP):
            sk = sink_ref[g * GQA_GROUP + hh] * LOG2E
            sh = s[hh * BLOCK:(hh + 1) * BLOCK]
            sh = jnp.concatenate([jnp.where(mask_prev, sh[:, :BLOCK], -jnp.inf), sh[:, BLOCK:2 * BLOCK],
                                  jnp.where(mask_next, sh[:, 2 * BLOCK:], -jnp.inf)], axis=1)
            m = jnp.maximum(jnp.max(sh, axis=-1, keepdims=True), sk)
            p = jnp.exp2(sh - m)
            denom = jnp.sum(p, axis=-1, keepdims=True) + jnp.exp2(sk - m)
            ps.append(p.astype(BF16))
            rs.append(1.0 / denom)
        p = jnp.concatenate(ps, axis=0)
        o = jnp.dot(p, vg, preferred_element_type=F32)
        oh = [o[hh * BLOCK:(hh + 1) * BLOCK] * rs[hh] for hh in range(GQA_GROUP)]
        return [jnp.where(low, oh[2 * pair], oh[2 * pair + 1]) for pair in range(GQA_GROUP // 2)]

    pending = [scores(*u) for u in units[:ATTN_LOOKAHEAD]]
    tiles = []
    for n, (j, g) in enumerate(units):
        if n + ATTN_LOOKAHEAD < len(units):
            pending.append(scores(*units[n + ATTN_LOOKAHEAD]))
        tiles += softmax_pv(j, g, pending.pop(0))
        if g == N_KV_HEADS - 1:
            a = jnp.concatenate(tiles, axis=1)
            o_ref[0, j * BLOCK:(j + 1) * BLOCK, :] = _rms(a, gain, ATTN_WIDTH).astype(BF16)
            tiles = []


def _attention(sink, q, k2, v2, out_gain):
    nb, s, _ = q.shape
    t = ATTN_TILE
    r = t // BLOCK
    last = s // BLOCK - 1
    main = lambda w: pl.BlockSpec((1, t, w), lambda b, i, *_: (b, i, 0))
    prev = pl.BlockSpec((1, BLOCK, 2 * KV_WIDTH), lambda b, i, *_: (b, jnp.maximum(i * r - 1, 0), 0))
    nxt = pl.BlockSpec((1, BLOCK, 2 * KV_WIDTH), lambda b, i, *_: (b, jnp.minimum((i + 1) * r, last), 0))
    grid_spec = pltpu.PrefetchScalarGridSpec(
        num_scalar_prefetch=1,
        grid=(nb, s // t),
        in_specs=[main(ATTN_WIDTH), prev, main(2 * KV_WIDTH), nxt, prev, main(2 * KV_WIDTH), nxt,
                  pl.BlockSpec((1, ATTN_WIDTH), lambda b, i, *_: (0, 0))],
        out_specs=main(ATTN_WIDTH),
    )
    return pl.pallas_call(
        functools.partial(_attn_kernel, seq=s),
        grid_spec=grid_spec,
        out_shape=jax.ShapeDtypeStruct((nb, s, ATTN_WIDTH), BF16),
        compiler_params=_COMPILER_PARAMS,
        name="attention",
    )(sink, q, k2, k2, k2, v2, v2, v2, out_gain)


def _dft1_kernel(m1_ref, a_ref, b_ref, y_ref):
    ab = jnp.concatenate([a_ref[0], b_ref[0]], axis=0)
    y = jnp.dot(m1_ref[...], ab, preferred_element_type=F32)
    y_ref[0] = y.astype(BF16)


def _dft_stage1(m1, a, b):
    nb = a.shape[0]
    cols = DFT_N * FOURIER_WIDTH
    a = a.reshape(nb, DFT_N, cols)
    b = b.reshape(nb, DFT_N, cols)
    spec = pl.BlockSpec((1, DFT_N, S1_COLS), lambda n, c: (n, 0, c))
    return pl.pallas_call(
        _dft1_kernel,
        grid=(nb, cols // S1_COLS),
        in_specs=[_const_spec((2 * DFT_N, 2 * DFT_N)), spec, spec],
        out_specs=pl.BlockSpec((1, 2 * DFT_N, S1_COLS), lambda n, c: (n, 0, c)),
        out_shape=jax.ShapeDtypeStruct((nb, 2 * DFT_N, cols), BF16),
        compiler_params=_COMPILER_PARAMS,
        name="dft_stage1",
    )(m1, a, b)


def _dft2_kernel(cn_ref, sn_ref, wc_ref, ws_ref, y_ref, wf_ref, g_ref, o_ref):
    cn = cn_ref[...]
    sn = sn_ref[...]
    wf = wf_ref[...]
    gain = g_ref[...]
    for j in range(S2_K1):
        wc = wc_ref[j:j + 1, :]
        ws = ws_ref[j:j + 1, :]
        gc = cn * wc - sn * ws
        gs = sn * wc + cn * ws
        g = jnp.concatenate([gc, gs], axis=1).astype(BF16)
        y = jnp.concatenate([y_ref[0, 0, j], y_ref[0, 1, j]], axis=0)
        xr = jnp.dot(g, y, preferred_element_type=F32)
        f = jnp.dot(xr.astype(BF16), wf, preferred_element_type=F32)
        o_ref[0, :, j * FOURIER_WIDTH:(j + 1) * FOURIER_WIDTH] = _rms(f, gain, FOURIER_WIDTH).astype(BF16)


def _dft_stage2(cn, sn, wc, ws, y, wf_bd, out_gain):
    nb = y.shape[0]
    y = y.reshape(nb, 2, DFT_N, DFT_N, FOURIER_WIDTH)
    tw = pl.BlockSpec((S2_K1, DFT_N), lambda n, k: (k, 0))
    out = pl.pallas_call(
        _dft2_kernel,
        grid=(nb, DFT_N // S2_K1),
        in_specs=[_const_spec((DFT_N, DFT_N)), _const_spec((DFT_N, DFT_N)), tw, tw,
                  pl.BlockSpec((1, 2, S2_K1, DFT_N, FOURIER_WIDTH), lambda n, k: (n, 0, k, 0, 0)),
                  _const_spec((FOURIER_WIDTH, FOURIER_WIDTH)), _const_spec((1, FOURIER_WIDTH))],
        out_specs=pl.BlockSpec((1, DFT_N, S2_K1 * FOURIER_WIDTH), lambda n, k: (n, 0, k)),
        out_shape=jax.ShapeDtypeStruct((nb, DFT_N, DFT_N * FOURIER_WIDTH), BF16),
        compiler_params=_COMPILER_PARAMS,
        name="dft_stage2",
    )(cn, sn, wc, ws, y, wf_bd, out_gain)
    return out.reshape(nb, DFT_N * DFT_N, FOURIER_WIDTH)


def _ffn_kernel(xm_ref, xp_ref, xn_ref, am_ref, ap_ref, an_ref, fm_ref, fp_ref, fn_ref, p_ref,
                wout_ref, gffn_ref, wup_ref, cw_ref, cb_ref, wdown_ref, wple_ref, gple_ref, wgate_ref,
                bgate_ref, o_ref):
    i = pl.program_id(1)
    nt = pl.num_programs(1)
    t = FFN_TILE
    ext = t + 2 * FFN_HALO
    x_ext = jnp.concatenate([xp_ref[0], xm_ref[0], xn_ref[0]], axis=0)
    a_ext = jnp.concatenate([ap_ref[0], am_ref[0], an_ref[0]], axis=0)
    f_ext = jnp.concatenate([fp_ref[0], fm_ref[0], fn_ref[0]], axis=0)
    mixed = jnp.concatenate([a_ext, f_ext], axis=1)
    x1 = x_ext + jnp.dot(mixed, wout_ref[...], preferred_element_type=F32)
    h = _rms(x1, gffn_ref[...], D_MODEL)
    row = lax.broadcasted_iota(jnp.int32, (ext, 1), 0)
    first_valid = jnp.where(i == 0, FFN_HALO, 0)
    end_valid = jnp.where(i == nt - 1, t + FFN_HALO, ext)
    h = jnp.where((row >= first_valid) & (row < end_valid), h, 0.0).astype(BF16)

    acc = jnp.zeros((t, D_MODEL), F32)
    def up_proj(j):
        return tuple(jnp.dot(h, wup_ref[:, c0:c0 + FFN_CHUNK], preferred_element_type=F32)
                     for c0 in (j * FFN_CHUNK, D_FF + j * FFN_CHUNK))

    def conv(hu, c0):
        cw = cw_ref[:, c0:c0 + FFN_CHUNK]
        before = pltpu.roll(hu, 1, 0)[FFN_HALO:FFN_HALO + t]
        after = pltpu.roll(hu, ext - 1, 0)[FFN_HALO:FFN_HALO + t]
        return (before * cw[0:1] + hu[FFN_HALO:FFN_HALO + t] * cw[1:2] + after * cw[2:3]
                + cb_ref[:, c0:c0 + FFN_CHUNK])

    pending = [up_proj(j) for j in range(FFN_LOOKAHEAD)]
    for j in range(N_FFN_CHUNKS):
        if j + FFN_LOOKAHEAD < N_FFN_CHUNKS:
            pending.append(up_proj(j + FFN_LOOKAHEAD))
        hu_gate, hu_up = pending.pop(0)
        gate = conv(hu_gate, j * FFN_CHUNK)
        up = conv(hu_up, D_FF + j * FFN_CHUNK)
        act = (jax.nn.silu(gate) * up).astype(BF16)
        acc = acc + jnp.dot(act, wdown_ref[j * FFN_CHUNK:(j + 1) * FFN_CHUNK, :], preferred_element_type=F32)

    x2 = x1[FFN_HALO:FFN_HALO + t] + acc
    e = jnp.dot(p_ref[0].astype(BF16), wple_ref[...], preferred_element_type=F32)
    e = _rms(e, gple_ref[...], D_MODEL)
    gate = jax.nn.sigmoid(jnp.dot(x2.astype(BF16), wgate_ref[...], preferred_element_type=F32) + bgate_ref[...])
    o_ref[0] = x2 + gate * e


def _ffn(x, a, f, p, w_out, ffn_gain, w_up, conv_w, conv_b, w_down, w_ple, ple_gain, w_gate, b_gate):
    nb, s, _ = x.shape
    t = FFN_TILE
    r = t // FFN_HALO
    last = s // FFN_HALO - 1

    def specs(w):
        return [pl.BlockSpec((1, t, w), lambda b, i: (b, i, 0)),
                pl.BlockSpec((1, FFN_HALO, w), lambda b, i: (b, jnp.maximum(i * r - 1, 0), 0)),
                pl.BlockSpec((1, FFN_HALO, w), lambda b, i: (b, jnp.minimum((i + 1) * r, last), 0))]

    in_specs = (specs(D_MODEL) + specs(ATTN_WIDTH) + specs(FOURIER_WIDTH)
                + [pl.BlockSpec((1, t, PLE_DIM), lambda b, i: (b, i, 0)),
                   _const_spec((D_MODEL, D_MODEL)), _const_spec((1, D_MODEL)),
                   _const_spec((D_MODEL, 2 * D_FF)), _const_spec((3, 2 * D_FF)), _const_spec((1, 2 * D_FF)),
                   _const_spec((D_FF, D_MODEL)), _const_spec((PLE_DIM, D_MODEL)), _const_spec((1, D_MODEL)),
                   _const_spec((D_MODEL, D_MODEL)), _const_spec((1, D_MODEL))])
    return pl.pallas_call(
        _ffn_kernel,
        grid=(nb, s // t),
        in_specs=in_specs,
        out_specs=pl.BlockSpec((1, t, D_MODEL), lambda b, i: (b, i, 0)),
        out_shape=jax.ShapeDtypeStruct((nb, s, D_MODEL), F32),
        compiler_params=_COMPILER_PARAMS,
        name="ffn",
    )(x, x, x, a, a, a, f, f, f, p, w_out, ffn_gain, w_up, conv_w, conv_b, w_down, w_ple, ple_gain,
      w_gate, b_gate)


def _dft_tables():
    n = np.arange(DFT_N)
    ang = 2.0 * np.pi * ((n[:, None] * n[None, :]) % DFT_N) / DFT_N
    cn, sn = np.cos(ang), np.sin(ang)
    m1 = np.block([[cn, -sn], [-sn, -cn]])
    seq = DFT_N * DFT_N
    tw = 2.0 * np.pi * ((n[:, None] * n[None, :]) % seq) / seq
    ortho = 1.0 / np.sqrt(seq * FOURIER_GROUP_DIM)
    c = np.arange(FOURIER_GROUP_DIM)
    ang_c = 2.0 * np.pi * ((c[:, None] * c[None, :]) % FOURIER_GROUP_DIM) / FOURIER_GROUP_DIM
    eye = np.eye(N_FOURIER_GROUPS)
    cs = np.concatenate([np.kron(eye, np.cos(ang_c)), np.kron(eye, np.sin(ang_c))], axis=1)
    hsum = np.kron(np.eye(LANES // HEAD_DIM), np.ones((HEAD_DIM, HEAD_DIM)))
    f32 = lambda a: jnp.asarray(a, dtype=F32)
    return (f32(m1), f32(cn * ortho), f32(sn * ortho), f32(np.cos(tw)), f32(np.sin(tw)), f32(cs), f32(hsum))


def _rope_tables(seq):
    inv_freq = ROPE_THETA ** (-jnp.arange(0, ROT_DIM, 2, dtype=F32) / ROT_DIM)
    inv_head = jnp.concatenate([inv_freq, inv_freq, jnp.zeros((HEAD_DIM - ROT_DIM,), F32)])
    inv_lane = jnp.concatenate([inv_head] * (LANES // HEAD_DIM))
    ang = jnp.arange(seq, dtype=F32)[:, None] * inv_lane[None, :]
    cos, sin = jnp.cos(ang), jnp.sin(ang)
    d = (jnp.arange(LANES) % HEAD_DIM)[None, :]
    sin_a = jnp.where(d < ROT_HALF, -sin, 0.0)
    sin_b = jnp.where((d >= ROT_HALF) & (d < ROT_DIM), sin, 0.0)
    return cos, sin_a, sin_b


def _block_diag(w):
    g, c, e = w.shape
    eye = jnp.eye(g, dtype=w.dtype)
    return (eye[:, None, :, None] * w[:, :, None, :]).reshape(g * c, g * e)


def _layer(x, p, prm, tables):
    (attn_norm, w_in, q_norm, k_norm, sink, w_fourier, attn_out_norm, fourier_out_norm, w_out, ffn_norm,
     w_up, conv_w, conv_b, w_down, w_ple, ple_norm, w_ple_gate, b_ple_gate) = prm
    m1, cn, sn, wc, ws, cs_bd, hsum = tables
    seq = x.shape[1]
    assert seq == DFT_N * DFT_N
    cos_t, sin_a, sin_b = _rope_tables(seq)
    row = lambda v: v.reshape(1, -1)
    two = lambda v: jnp.concatenate([v] * (LANES // HEAD_DIM)).reshape(1, LANES)

    q, k2, v2, a_dft, b_dft = _inproj(x, row(attn_norm), w_in.astype(BF16), two(q_norm), two(k_norm),
                                      cos_t, sin_a, sin_b, hsum.astype(BF16), cs_bd.astype(BF16))
    attn = _attention(sink, q, k2, v2, row(attn_out_norm))
    y = _dft_stage1(m1.astype(BF16), a_dft, b_dft)
    four = _dft_stage2(cn, sn, wc, ws, y, _block_diag(w_fourier).astype(BF16), row(fourier_out_norm))
    return _ffn(x, attn, four, p, w_out.astype(BF16), row(ffn_norm), w_up.astype(BF16), conv_w, row(conv_b),
                w_down.astype(BF16), w_ple.astype(BF16), row(ple_norm), w_ple_gate.astype(BF16),
                row(b_ple_gate))


def kernel(x_prompt, x_sample, p_prompt, p_sample, attn_norm, w_in, q_norm, k_norm, sink, w_fourier,
           attn_out_norm, fourier_out_norm, w_out, ffn_norm, w_up, conv_w, conv_b, w_down, w_ple, ple_norm,
           w_ple_gate, b_ple_gate):
    stacked = (attn_norm, w_in, q_norm, k_norm, sink, w_fourier, attn_out_norm, fourier_out_norm, w_out,
               ffn_norm, w_up, conv_w, conv_b, w_down, w_ple, ple_norm, w_ple_gate, b_ple_gate)
    tables = _dft_tables()
    y_prompt, y_sample = x_prompt, x_sample
    for i in range(attn_norm.shape[0]):
        prm = tuple(w[i] for w in stacked)
        y_prompt = _layer(y_prompt, p_prompt[i], prm, tables)
        y_sample = _layer(y_sample, p_sample[i], prm, tables)
    return (y_prompt, y_sample)
```

```python
import functools

import numpy as np
import jax
import jax.numpy as jnp
from jax import lax
from jax.experimental import pallas as pl
from jax.experimental.pallas import tpu as pltpu

F32 = jnp.float32
BF16 = jnp.bfloat16

D_MODEL = 1024
HEAD_DIM = 64
N_HEADS = 8
N_KV_HEADS = 2
GQA_GROUP = N_HEADS // N_KV_HEADS
ATTN_WIDTH = N_HEADS * HEAD_DIM
KV_WIDTH = N_KV_HEADS * HEAD_DIM
FOURIER_WIDTH = 512
FOURIER_GROUP_DIM = 64
N_FOURIER_GROUPS = FOURIER_WIDTH // FOURIER_GROUP_DIM
IN_WIDTH = ATTN_WIDTH + 2 * KV_WIDTH + FOURIER_WIDTH
BLOCK = 128
ROPE_THETA = 500000.0
ROT_DIM = HEAD_DIM // 4
ROT_HALF = ROT_DIM // 2
D_FF = 2816
PLE_DIM = 256
EPS = 1e-6
LOG2E = 1.4426950408889634

LANES = 128
BF16_ROWS = 16
VMEM_LIMIT_BYTES = 56 * 1024 * 1024

DFT_N = 128

IN_TILE = 512
ATTN_TILE = 512
ATTN_LOOKAHEAD = 2
FFN_LOOKAHEAD = 2
F32_ROWS = 8
S1_GROUP = F32_ROWS
AB_SLABS = 2 * FOURIER_WIDTH // LANES
F_SLABS = FOURIER_WIDTH // LANES
S2_K1 = F32_ROWS
FFN_TILE = 256
FFN_HALO = BF16_ROWS
FFN_CHUNK = 256
N_FFN_CHUNKS = D_FF // FFN_CHUNK


_COMPILER_PARAMS = pltpu.CompilerParams(
    dimension_semantics=("parallel", "parallel"),
    vmem_limit_bytes=VMEM_LIMIT_BYTES,
)


def _rms(x, gain, n):
    ms = jnp.sum(x * x, axis=-1, keepdims=True) * (1.0 / n)
    return x * lax.rsqrt(ms + EPS) * gain


def _const_spec(shape):
    zeros = (0,) * len(shape)
    return pl.BlockSpec(shape, lambda *_: zeros, pipeline_mode=pl.Buffered(1))


def _inproj_kernel(x_ref, g_ref, win_ref, qg_ref, kg_ref, cos_ref, sina_ref, sinb_ref, hsum_ref, cs_ref,
                   q_ref, k2_ref, v2_ref, ab_ref):
    x = x_ref[0]
    h = _rms(x, g_ref[...], D_MODEL).astype(BF16)
    z = jnp.dot(h, win_ref[...], preferred_element_type=F32)
    cos_t = cos_ref[...]
    sin_a = sina_ref[...]
    sin_b = sinb_ref[...]
    hsum = hsum_ref[...]

    def norm_rope(t, gain):
        ssq = jnp.dot((t * t).astype(BF16), hsum, preferred_element_type=F32)
        tn = t * lax.rsqrt(ssq * (1.0 / HEAD_DIM) + EPS) * gain
        up = pltpu.roll(tn, LANES - ROT_HALF, 1)
        dn = pltpu.roll(tn, ROT_HALF, 1)
        return tn * cos_t + up * sin_a + dn * sin_b

    scale = HEAD_DIM ** -0.5 * LOG2E
    for t in range(ATTN_WIDTH // LANES):
        qt = norm_rope(z[:, t * LANES:(t + 1) * LANES], qg_ref[...])
        q_ref[0, :, t * LANES:(t + 1) * LANES] = (qt * scale).astype(BF16)

    lane = lax.broadcasted_iota(jnp.int32, (x.shape[0], LANES), 1)
    low = lane < HEAD_DIM

    def dup_heads(t):
        r = pltpu.roll(t, HEAD_DIM, 1)
        return jnp.concatenate([jnp.where(low, t, r), jnp.where(low, r, t)], axis=1)

    k = norm_rope(z[:, ATTN_WIDTH:ATTN_WIDTH + KV_WIDTH], kg_ref[...])
    k2_ref[0] = dup_heads(k).astype(BF16)
    v = z[:, ATTN_WIDTH + KV_WIDTH:ATTN_WIDTH + 2 * KV_WIDTH]
    v2_ref[0] = dup_heads(v).astype(BF16)

    u = z[:, ATTN_WIDTH + 2 * KV_WIDTH:].astype(BF16)
    ab = jnp.dot(u, cs_ref[...], preferred_element_type=F32)
    for t in range(AB_SLABS):
        ab_ref[0, t] = ab[:, t * LANES:(t + 1) * LANES]


def _inproj(x, attn_norm, w_in, q_gain, k_gain, cos_t, sin_a, sin_b, hsum, cs_bd):
    nb, s, _ = x.shape
    t = IN_TILE
    tok = lambda w: pl.BlockSpec((1, t, w), lambda b, i: (b, i, 0))
    pos = pl.BlockSpec((t, LANES), lambda b, i: (i, 0))
    out_shape = (
        jax.ShapeDtypeStruct((nb, s, ATTN_WIDTH), BF16),
        jax.ShapeDtypeStruct((nb, s, 2 * KV_WIDTH), BF16),
        jax.ShapeDtypeStruct((nb, s, 2 * KV_WIDTH), BF16),
        jax.ShapeDtypeStruct((nb, AB_SLABS, s, LANES), F32),
    )
    return pl.pallas_call(
        _inproj_kernel,
        grid=(nb, s // t),
        in_specs=[tok(D_MODEL), _const_spec((1, D_MODEL)), _const_spec((D_MODEL, IN_WIDTH)),
                  _const_spec((1, LANES)), _const_spec((1, LANES)), pos, pos, pos,
                  _const_spec((LANES, LANES)), _const_spec((FOURIER_WIDTH, 2 * FOURIER_WIDTH))],
        out_specs=(tok(ATTN_WIDTH), tok(2 * KV_WIDTH), tok(2 * KV_WIDTH),
                   pl.BlockSpec((1, AB_SLABS, t, LANES), lambda b, i: (b, 0, i, 0))),
        out_shape=out_shape,
        compiler_params=_COMPILER_PARAMS,
        name="inproj",
    )(x, attn_norm, w_in, q_gain, k_gain, cos_t, sin_a, sin_b, hsum, cs_bd)


def _attn_kernel(sink_ref, q_ref, kp_ref, km_ref, kn_ref, vp_ref, vm_ref, vn_ref, g_ref, o_ref, *, seq):
    i = pl.program_id(1)
    kcat = jnp.concatenate([kp_ref[0], km_ref[0], kn_ref[0]], axis=0)
    vcat = jnp.concatenate([vp_ref[0], vm_ref[0], vn_ref[0]], axis=0)
    nkeys = 3 * BLOCK
    diff = (lax.broadcasted_iota(jnp.int32, (BLOCK, BLOCK), 1)
            - lax.broadcasted_iota(jnp.int32, (BLOCK, BLOCK), 0))
    lane = lax.broadcasted_iota(jnp.int32, (BLOCK, LANES), 1)
    low = lane < HEAD_DIM
    gain = g_ref[...]
    units = [(j, g) for j in range(ATTN_TILE // BLOCK) for g in range(N_KV_HEADS)]

    def scores(j, g):
        kg = kcat[j * BLOCK:j * BLOCK + nkeys, g * LANES:(g + 1) * LANES]
        qs = []
        for hh in range(GQA_GROUP):
            t = (g * GQA_GROUP + hh) // 2
            qt = q_ref[0, j * BLOCK:(j + 1) * BLOCK, t * LANES:(t + 1) * LANES]
            sel = low if hh % 2 == 0 else jnp.logical_not(low)
            qs.append(jnp.where(sel, qt, jnp.zeros_like(qt)))
        qg = jnp.concatenate(qs, axis=0)
        return lax.dot_general(qg, kg, (((1,), (1,)), ((), ())), preferred_element_type=F32)

    def softmax_pv(j, g, s):
        qblk = i * (ATTN_TILE // BLOCK) + j
        lo = jnp.where(qblk > 0, 0, BLOCK)
        hi = jnp.where(qblk < seq // BLOCK - 1, 0, -BLOCK)
        mask_prev = diff >= lo
        mask_next = diff <= hi
        vg = vcat[j * BLOCK:j * BLOCK + nkeys, g * LANES:(g + 1) * LANES]
        ps, rs = [], []
        for hh in range(GQA_GROUP):
            sk = sink_ref[g * GQA_GROUP + hh] * LOG2E
            sh = s[hh * BLOCK:(hh + 1) * BLOCK]
            sh = jnp.concatenate([jnp.where(mask_prev, sh[:, :BLOCK], -jnp.inf), sh[:, BLOCK:2 * BLOCK],
                                  jnp.where(mask_next, sh[:, 2 * BLOCK:], -jnp.inf)], axis=1)
            m = jnp.maximum(jnp.max(sh, axis=-1, keepdims=True), sk)
            p = jnp.exp2(sh - m)
            denom = jnp.sum(p, axis=-1, keepdims=True) + jnp.exp2(sk - m)
            ps.append(p.astype(BF16))
            rs.append(1.0 / denom)
        p = jnp.concatenate(ps, axis=0)
        o = jnp.dot(p, vg, preferred_element_type=F32)
        oh = [o[hh * BLOCK:(hh + 1) * BLOCK] * rs[hh] for hh in range(GQA_GROUP)]
        return [jnp.where(low, oh[2 * pair], oh[2 * pair + 1]) for pair in range(GQA_GROUP // 2)]

    pending = [scores(*u) for u in units[:ATTN_LOOKAHEAD]]
    tiles = []
    for n, (j, g) in enumerate(units):
        if n + ATTN_LOOKAHEAD < len(units):
            pending.append(scores(*units[n + ATTN_LOOKAHEAD]))
        tiles += softmax_pv(j, g, pending.pop(0))
        if g == N_KV_HEADS - 1:
            a = jnp.concatenate(tiles, axis=1)
            o_ref[0, j * BLOCK:(j + 1) * BLOCK, :] = _rms(a, gain, ATTN_WIDTH).astype(BF16)
            tiles = []


def _attention(sink, q, k2, v2, out_gain):
    nb, s, _ = q.shape
    t = ATTN_TILE
    r = t // BLOCK
    last = s // BLOCK - 1
    main = lambda w: pl.BlockSpec((1, t, w), lambda b, i, *_: (b, i, 0))
    prev = pl.BlockSpec((1, BLOCK, 2 * KV_WIDTH), lambda b, i, *_: (b, jnp.maximum(i * r - 1, 0), 0))
    nxt = pl.BlockSpec((1, BLOCK, 2 * KV_WIDTH), lambda b, i, *_: (b, jnp.minimum((i + 1) * r, last), 0))
    grid_spec = pltpu.PrefetchScalarGridSpec(
        num_scalar_prefetch=1,
        grid=(nb, s // t),
        in_specs=[main(ATTN_WIDTH), prev, main(2 * KV_WIDTH), nxt, prev, main(2 * KV_WIDTH), nxt,
                  pl.BlockSpec((1, ATTN_WIDTH), lambda b, i, *_: (0, 0))],
        out_specs=main(ATTN_WIDTH),
    )
    return pl.pallas_call(
        functools.partial(_attn_kernel, seq=s),
        grid_spec=grid_spec,
        out_shape=jax.ShapeDtypeStruct((nb, s, ATTN_WIDTH), BF16),
        compiler_params=_COMPILER_PARAMS,
        name="attention",
    )(sink, q, k2, k2, k2, v2, v2, v2, out_gain)


def _dft1_kernel(m1_ref, ab_ref, y_ref):
    m1 = m1_ref[...]
    for j in range(S1_GROUP):
        slabs = []
        for t in range(AB_SLABS):
            rows = ab_ref.at[0, t].reshape(DFT_N * S1_GROUP, LANES)
            slabs.append(rows[pl.ds(j, DFT_N, stride=S1_GROUP), :].astype(BF16))
        half = AB_SLABS // 2
        ab = jnp.concatenate([jnp.concatenate(slabs[:half], axis=1),
                              jnp.concatenate(slabs[half:], axis=1)], axis=0)
        y = jnp.dot(m1, ab, preferred_element_type=F32)
        y_ref[0, :, j * FOURIER_WIDTH:(j + 1) * FOURIER_WIDTH] = y.astype(BF16)


def _dft_stage1(m1, ab):
    nb = ab.shape[0]
    groups = DFT_N // S1_GROUP
    ab = ab.reshape(nb, AB_SLABS, DFT_N, groups, S1_GROUP, LANES)
    cols = S1_GROUP * FOURIER_WIDTH
    return pl.pallas_call(
        _dft1_kernel,
        grid=(nb, groups),
        in_specs=[_const_spec((2 * DFT_N, 2 * DFT_N)),
                  pl.BlockSpec((1, AB_SLABS, DFT_N, 1, S1_GROUP, LANES), lambda n, g: (n, 0, 0, g, 0, 0))],
        out_specs=pl.BlockSpec((1, 2 * DFT_N, cols), lambda n, g: (n, 0, g)),
        out_shape=jax.ShapeDtypeStruct((nb, 2 * DFT_N, DFT_N * FOURIER_WIDTH), BF16),
        compiler_params=_COMPILER_PARAMS,
        name="dft_stage1",
    )(m1, ab)


def _dft2_kernel(cn_ref, sn_ref, wc_ref, ws_ref, y_ref, wf_ref, g_ref, o_ref):
    cn = cn_ref[...]
    sn = sn_ref[...]
    wf = wf_ref[...]
    gain = g_ref[...]
    for j in range(S2_K1):
        wc = wc_ref[j:j + 1, :]
        ws = ws_ref[j:j + 1, :]
        gc = cn * wc - sn * ws
        gs = sn * wc + cn * ws
        g = jnp.concatenate([gc, gs], axis=1).astype(BF16)
        y = jnp.concatenate([y_ref[0, 0, j], y_ref[0, 1, j]], axis=0)
        xr = jnp.dot(g, y, preferred_element_type=F32)
        f = jnp.dot(xr.astype(BF16), wf, preferred_element_type=F32)
        fn = _rms(f, gain, FOURIER_WIDTH)
        for t in range(F_SLABS):
            rows = o_ref.at[0, t].reshape(DFT_N * S2_K1, LANES)
            rows[pl.ds(j, DFT_N, stride=S2_K1), :] = fn[:, t * LANES:(t + 1) * LANES]


def _dft_stage2(cn, sn, wc, ws, y, wf_bd, out_gain):
    nb = y.shape[0]
    y = y.reshape(nb, 2, DFT_N, DFT_N, FOURIER_WIDTH)
    groups = DFT_N // S2_K1
    tw = pl.BlockSpec((S2_K1, DFT_N), lambda n, k: (k, 0))
    out = pl.pallas_call(
        _dft2_kernel,
        grid=(nb, DFT_N // S2_K1),
        in_specs=[_const_spec((DFT_N, DFT_N)), _const_spec((DFT_N, DFT_N)), tw, tw,
                  pl.BlockSpec((1, 2, S2_K1, DFT_N, FOURIER_WIDTH), lambda n, k: (n, 0, k, 0, 0)),
                  _const_spec((FOURIER_WIDTH, FOURIER_WIDTH)), _const_spec((1, FOURIER_WIDTH))],
        out_specs=pl.BlockSpec((1, F_SLABS, DFT_N, 1, S2_K1, LANES), lambda n, k: (n, 0, 0, k, 0, 0)),
        out_shape=jax.ShapeDtypeStruct((nb, F_SLABS, DFT_N, groups, S2_K1, LANES), F32),
        compiler_params=_COMPILER_PARAMS,
        name="dft_stage2",
    )(cn, sn, wc, ws, y, wf_bd, out_gain)
    return out.reshape(nb, F_SLABS, DFT_N * DFT_N, LANES)


def _ffn_kernel(xm_ref, xp_ref, xn_ref, am_ref, ap_ref, an_ref, fm_ref, fp_ref, fn_ref, p_ref,
                wout_ref, gffn_ref, wup_ref, cw_ref, cb_ref, wdown_ref, wple_ref, gple_ref, wgate_ref,
                bgate_ref, o_ref):
    i = pl.program_id(1)
    nt = pl.num_programs(1)
    t = FFN_TILE
    ext = t + 2 * FFN_HALO
    x_ext = jnp.concatenate([xp_ref[0], xm_ref[0], xn_ref[0]], axis=0)
    a_ext = jnp.concatenate([ap_ref[0], am_ref[0], an_ref[0]], axis=0)
    f_ext = [jnp.concatenate([fp_ref[0, s], fm_ref[0, s], fn_ref[0, s]], axis=0).astype(BF16)
             for s in range(F_SLABS)]
    mixed = jnp.concatenate([a_ext] + f_ext, axis=1)
    x1 = x_ext + jnp.dot(mixed, wout_ref[...], preferred_element_type=F32)
    h = _rms(x1, gffn_ref[...], D_MODEL)
    row = lax.broadcasted_iota(jnp.int32, (ext, 1), 0)
    first_valid = jnp.where(i == 0, FFN_HALO, 0)
    end_valid = jnp.where(i == nt - 1, t + FFN_HALO, ext)
    h = jnp.where((row >= first_valid) & (row < end_valid), h, 0.0).astype(BF16)

    acc = jnp.zeros((t, D_MODEL), F32)
    def up_proj(j):
        return tuple(jnp.dot(h, wup_ref[:, c0:c0 + FFN_CHUNK], preferred_element_type=F32)
                     for c0 in (j * FFN_CHUNK, D_FF + j * FFN_CHUNK))

    def conv(hu, c0):
        cw = cw_ref[:, c0:c0 + FFN_CHUNK]
        before = pltpu.roll(hu, 1, 0)[FFN_HALO:FFN_HALO + t]
        after = pltpu.roll(hu, ext - 1, 0)[FFN_HALO:FFN_HALO + t]
        return (before * cw[0:1] + hu[FFN_HALO:FFN_HALO + t] * cw[1:2] + after * cw[2:3]
                + cb_ref[:, c0:c0 + FFN_CHUNK])

    pending = [up_proj(j) for j in range(FFN_LOOKAHEAD)]
    for j in range(N_FFN_CHUNKS):
        if j + FFN_LOOKAHEAD < N_FFN_CHUNKS:
            pending.append(up_proj(j + FFN_LOOKAHEAD))
        hu_gate, hu_up = pending.pop(0)
        gate = conv(hu_gate, j * FFN_CHUNK)
        up = conv(hu_up, D_FF + j * FFN_CHUNK)
        act = (jax.nn.silu(gate) * up).astype(BF16)
        acc = acc + jnp.dot(act, wdown_ref[j * FFN_CHUNK:(j + 1) * FFN_CHUNK, :], preferred_element_type=F32)

    x2 = x1[FFN_HALO:FFN_HALO + t] + acc
    e = jnp.dot(p_ref[0].astype(BF16), wple_ref[...], preferred_element_type=F32)
    e = _rms(e, gple_ref[...], D_MODEL)
    gate = jax.nn.sigmoid(jnp.dot(x2.astype(BF16), wgate_ref[...], preferred_element_type=F32) + bgate_ref[...])
    o_ref[0] = x2 + gate * e


def _ffn(x, a, f, p, w_out, ffn_gain, w_up, conv_w, conv_b, w_down, w_ple, ple_gain, w_gate, b_gate):
    nb, s, _ = x.shape
    t = FFN_TILE
    r = t // FFN_HALO
    last = s // FFN_HALO - 1

    def specs(w):
        return [pl.BlockSpec((1, t, w), lambda b, i: (b, i, 0)),
                pl.BlockSpec((1, FFN_HALO, w), lambda b, i: (b, jnp.maximum(i * r - 1, 0), 0)),
                pl.BlockSpec((1, FFN_HALO, w), lambda b, i: (b, jnp.minimum((i + 1) * r, last), 0))]

    f_specs = [pl.BlockSpec((1, F_SLABS, t, LANES), lambda b, i: (b, 0, i, 0)),
               pl.BlockSpec((1, F_SLABS, FFN_HALO, LANES), lambda b, i: (b, 0, jnp.maximum(i * r - 1, 0), 0)),
               pl.BlockSpec((1, F_SLABS, FFN_HALO, LANES), lambda b, i: (b, 0, jnp.minimum((i + 1) * r, last), 0))]

    in_specs = (specs(D_MODEL) + specs(ATTN_WIDTH) + f_specs
                + [pl.BlockSpec((1, t, PLE_DIM), lambda b, i: (b, i, 0)),
                   _const_spec((D_MODEL, D_MODEL)), _const_spec((1, D_MODEL)),
                   _const_spec((D_MODEL, 2 * D_FF)), _const_spec((3, 2 * D_FF)), _const_spec((1, 2 * D_FF)),
                   _const_spec((D_FF, D_MODEL)), _const_spec((PLE_DIM, D_MODEL)), _const_spec((1, D_MODEL)),
                   _const_spec((D_MODEL, D_MODEL)), _const_spec((1, D_MODEL))])
    return pl.pallas_call(
        _ffn_kernel,
        grid=(nb, s // t),
        in_specs=in_specs,
        out_specs=pl.BlockSpec((1, t, D_MODEL), lambda b, i: (b, i, 0)),
        out_shape=jax.ShapeDtypeStruct((nb, s, D_MODEL), F32),
        compiler_params=_COMPILER_PARAMS,
        name="ffn",
    )(x, x, x, a, a, a, f, f, f, p, w_out, ffn_gain, w_up, conv_w, conv_b, w_down, w_ple, ple_gain,
      w_gate, b_gate)


def _dft_tables():
    n = np.arange(DFT_N)
    ang = 2.0 * np.pi * ((n[:, None] * n[None, :]) % DFT_N) / DFT_N
    cn, sn = np.cos(ang), np.sin(ang)
    m1 = np.block([[cn, -sn], [-sn, -cn]])
    seq = DFT_N * DFT_N
    tw = 2.0 * np.pi * ((n[:, None] * n[None, :]) % seq) / seq
    ortho = 1.0 / np.sqrt(seq * FOURIER_GROUP_DIM)
    c = np.arange(FOURIER_GROUP_DIM)
    ang_c = 2.0 * np.pi * ((c[:, None] * c[None, :]) % FOURIER_GROUP_DIM) / FOURIER_GROUP_DIM
    eye = np.eye(N_FOURIER_GROUPS)
    cs = np.concatenate([np.kron(eye, np.cos(ang_c)), np.kron(eye, np.sin(ang_c))], axis=1)
    hsum = np.kron(np.eye(LANES // HEAD_DIM), np.ones((HEAD_DIM, HEAD_DIM)))
    f32 = lambda a: jnp.asarray(a, dtype=F32)
    return (f32(m1), f32(cn * ortho), f32(sn * ortho), f32(np.cos(tw)), f32(np.sin(tw)), f32(cs), f32(hsum))


def _rope_tables(seq):
    inv_freq = ROPE_THETA ** (-jnp.arange(0, ROT_DIM, 2, dtype=F32) / ROT_DIM)
    inv_head = jnp.concatenate([inv_freq, inv_freq, jnp.zeros((HEAD_DIM - ROT_DIM,), F32)])
    inv_lane = jnp.concatenate([inv_head] * (LANES // HEAD_DIM))
    ang = jnp.arange(seq, dtype=F32)[:, None] * inv_lane[None, :]
    cos, sin = jnp.cos(ang), jnp.sin(ang)
    d = (jnp.arange(LANES) % HEAD_DIM)[None, :]
    sin_a = jnp.where(d < ROT_HALF, -sin, 0.0)
    sin_b = jnp.where((d >= ROT_HALF) & (d < ROT_DIM), sin, 0.0)
    return cos, sin_a, sin_b


def _block_diag(w):
    g, c, e = w.shape
    eye = jnp.eye(g, dtype=w.dtype)
    return (eye[:, None, :, None] * w[:, :, None, :]).reshape(g * c, g * e)


def _layer(x, p, prm, tables):
    (attn_norm, w_in, q_norm, k_norm, sink, w_fourier, attn_out_norm, fourier_out_norm, w_out, ffn_norm,
     w_up, conv_w, conv_b, w_down, w_ple, ple_norm, w_ple_gate, b_ple_gate) = prm
    m1, cn, sn, wc, ws, cs_bd, hsum = tables
    seq = x.shape[1]
    assert seq == DFT_N * DFT_N
    cos_t, sin_a, sin_b = _rope_tables(seq)
    row = lambda v: v.reshape(1, -1)
    two = lambda v: jnp.concatenate([v] * (LANES // HEAD_DIM)).reshape(1, LANES)

    q, k2, v2, ab = _inproj(x, row(attn_norm), w_in.astype(BF16), two(q_norm), two(k_norm),
                            cos_t, sin_a, sin_b, hsum.astype(BF16), cs_bd.astype(BF16))
    attn = _attention(sink, q, k2, v2, row(attn_out_norm))
    y = _dft_stage1(m1.astype(BF16), ab)
    four = _dft_stage2(cn, sn, wc, ws, y, _block_diag(w_fourier).astype(BF16), row(fourier_out_norm))
    return _ffn(x, attn, four, p, w_out.astype(BF16), row(ffn_norm), w_up.astype(BF16), conv_w, row(conv_b),
                w_down.astype(BF16), w_ple.astype(BF16), row(ple_norm), w_ple_gate.astype(BF16),
                row(b_ple_gate))


def kernel(x_prompt, x_sample, p_prompt, p_sample, attn_norm, w_in, q_norm, k_norm, sink, w_fourier,
           attn_out_norm, fourier_out_norm, w_out, ffn_norm, w_up, conv_w, conv_b, w_down, w_ple, ple_norm,
           w_ple_gate, b_ple_gate):
    stacked = (attn_norm, w_in, q_norm, k_norm, sink, w_fourier, attn_out_norm, fourier_out_norm, w_out,
               ffn_norm, w_up, conv_w, conv_b, w_down, w_ple, ple_norm, w_ple_gate, b_ple_gate)
    tables = _dft_tables()
    y_prompt, y_sample = x_prompt, x_sample
    for i in range(attn_norm.shape[0]):
        prm = tuple(w[i] for w in stacked)
        y_prompt = _layer(y_prompt, p_prompt[i], prm, tables)
        y_sample = _layer(y_sample, p_sample[i], prm, tables)
    return (y_prompt, y_sample)
```

```python
import functools

import numpy as np
import jax
import jax.numpy as jnp
from jax import lax
from jax.experimental import pallas as pl
from jax.experimental.pallas import tpu as pltpu

F32 = jnp.float32
BF16 = jnp.bfloat16

D_MODEL = 1024
HEAD_DIM = 64
N_HEADS = 8
N_KV_HEADS = 2
GQA_GROUP = N_HEADS // N_KV_HEADS
ATTN_WIDTH = N_HEADS * HEAD_DIM
KV_WIDTH = N_KV_HEADS * HEAD_DIM
FOURIER_WIDTH = 512
FOURIER_GROUP_DIM = 64
N_FOURIER_GROUPS = FOURIER_WIDTH // FOURIER_GROUP_DIM
IN_WIDTH = ATTN_WIDTH + 2 * KV_WIDTH + FOURIER_WIDTH
BLOCK = 128
ROPE_THETA = 500000.0
ROT_DIM = HEAD_DIM // 4
ROT_HALF = ROT_DIM // 2
D_FF = 2816
PLE_DIM = 256
EPS = 1e-6
LOG2E = 1.4426950408889634

LANES = 128
MXU_DIM = 256
BF16_ROWS = 16
VMEM_LIMIT_BYTES = 56 * 1024 * 1024

DFT_N = 128

IN_TILE = 1024
ATTN_TILE = 512
ATTN_LOOKAHEAD = 2
FFN_LOOKAHEAD = 2
F32_ROWS = 8
S1_GROUP = F32_ROWS
AB_SLABS = 2 * FOURIER_WIDTH // LANES
F_SLABS = FOURIER_WIDTH // LANES
S2_K1 = F32_ROWS
FFN_TILE = 256
FFN_HALO = BF16_ROWS
UP_HALO = F32_ROWS
FFN_CHUNK = 256
N_FFN_CHUNKS = D_FF // FFN_CHUNK


_COMPILER_PARAMS = pltpu.CompilerParams(
    dimension_semantics=("parallel", "parallel"),
    vmem_limit_bytes=VMEM_LIMIT_BYTES,
)


def _rms(x, gain, n):
    ms = jnp.sum(x * x, axis=-1, keepdims=True) * (1.0 / n)
    return x * lax.rsqrt(ms + EPS) * gain


def _const_spec(shape):
    zeros = (0,) * len(shape)
    return pl.BlockSpec(shape, lambda *_: zeros, pipeline_mode=pl.Buffered(1))


def _inproj_kernel(x_ref, g_ref, win_ref, qg_ref, kg_ref, cos_ref, sina_ref, sinb_ref, hsum_ref, cs_ref,
                   q_ref, k2_ref, v2_ref, ab_ref):
    x = x_ref[0]
    h = _rms(x, g_ref[...], D_MODEL).astype(BF16)
    z = jnp.dot(h, win_ref[...], preferred_element_type=F32)
    cos_t = cos_ref[...]
    sin_a = sina_ref[...]
    sin_b = sinb_ref[...]

    def norm_rope(t, gain):
        w = t.shape[1]
        wide = lambda a: jnp.concatenate([a] * (w // LANES), axis=1)
        ssq = jnp.dot((t * t).astype(BF16), hsum_ref[:w, :w], preferred_element_type=F32)
        tn = t * lax.rsqrt(ssq * (1.0 / HEAD_DIM) + EPS) * wide(gain)
        up = pltpu.roll(tn, w - ROT_HALF, 1)
        dn = pltpu.roll(tn, ROT_HALF, 1)
        return tn * wide(cos_t) + up * wide(sin_a) + dn * wide(sin_b)

    scale = HEAD_DIM ** -0.5 * LOG2E
    for t in range(ATTN_WIDTH // MXU_DIM):
        qt = norm_rope(z[:, t * MXU_DIM:(t + 1) * MXU_DIM], qg_ref[...])
        q_ref[0, :, t * MXU_DIM:(t + 1) * MXU_DIM] = (qt * scale).astype(BF16)

    lane = lax.broadcasted_iota(jnp.int32, (x.shape[0], LANES), 1)
    low = lane < HEAD_DIM

    def dup_heads(t):
        r = pltpu.roll(t, HEAD_DIM, 1)
        return jnp.concatenate([jnp.where(low, t, r), jnp.where(low, r, t)], axis=1)

    k = norm_rope(z[:, ATTN_WIDTH:ATTN_WIDTH + KV_WIDTH], kg_ref[...])
    k2_ref[0] = dup_heads(k).astype(BF16)
    v = z[:, ATTN_WIDTH + KV_WIDTH:ATTN_WIDTH + 2 * KV_WIDTH]
    v2_ref[0] = dup_heads(v).astype(BF16)

    u = z[:, ATTN_WIDTH + 2 * KV_WIDTH:].astype(BF16)
    halves = [jnp.dot(u[:, c * MXU_DIM:(c + 1) * MXU_DIM], cs_ref[...], preferred_element_type=F32)
              for c in range(FOURIER_WIDTH // MXU_DIM)]
    per_half = MXU_DIM // LANES
    for part in range(2):
        for c, ab in enumerate(halves):
            for t in range(per_half):
                lo = part * MXU_DIM + t * LANES
                ab_ref[0, part * F_SLABS + c * per_half + t] = ab[:, lo:lo + LANES]


def _inproj(x, attn_norm, w_in, q_gain, k_gain, cos_t, sin_a, sin_b, hsum, cs_bd):
    nb, s, _ = x.shape
    t = IN_TILE
    tok = lambda w: pl.BlockSpec((1, t, w), lambda b, i: (b, i, 0))
    pos = pl.BlockSpec((t, LANES), lambda b, i: (i, 0))
    out_shape = (
        jax.ShapeDtypeStruct((nb, s, ATTN_WIDTH), BF16),
        jax.ShapeDtypeStruct((nb, s, 2 * KV_WIDTH), BF16),
        jax.ShapeDtypeStruct((nb, s, 2 * KV_WIDTH), BF16),
        jax.ShapeDtypeStruct((nb, AB_SLABS, s, LANES), F32),
    )
    return pl.pallas_call(
        _inproj_kernel,
        grid=(nb, s // t),
        in_specs=[tok(D_MODEL), _const_spec((1, D_MODEL)), _const_spec((D_MODEL, IN_WIDTH)),
                  _const_spec((1, LANES)), _const_spec((1, LANES)), pos, pos, pos,
                  _const_spec((MXU_DIM, MXU_DIM)), _const_spec((MXU_DIM, 2 * MXU_DIM))],
        out_specs=(tok(ATTN_WIDTH), tok(2 * KV_WIDTH), tok(2 * KV_WIDTH),
                   pl.BlockSpec((1, AB_SLABS, t, LANES), lambda b, i: (b, 0, i, 0))),
        out_shape=out_shape,
        compiler_params=_COMPILER_PARAMS,
        name="inproj",
    )(x, attn_norm, w_in, q_gain, k_gain, cos_t, sin_a, sin_b, hsum, cs_bd)


def _attn_kernel(sink_ref, q_ref, kp_ref, km_ref, kn_ref, vp_ref, vm_ref, vn_ref, g_ref, o_ref, *, seq):
    i = pl.program_id(1)
    kcat = jnp.concatenate([kp_ref[0], km_ref[0], kn_ref[0]], axis=0)
    vcat = jnp.concatenate([vp_ref[0], vm_ref[0], vn_ref[0]], axis=0)
    nkeys = 3 * BLOCK
    diff = (lax.broadcasted_iota(jnp.int32, (BLOCK, BLOCK), 1)
            - lax.broadcasted_iota(jnp.int32, (BLOCK, BLOCK), 0))
    lane = lax.broadcasted_iota(jnp.int32, (BLOCK, LANES), 1)
    low = lane < HEAD_DIM
    gain = g_ref[...]
    units = [(j, g) for j in range(ATTN_TILE // BLOCK) for g in range(N_KV_HEADS)]

    def scores(j, g):
        kg = kcat[j * BLOCK:j * BLOCK + nkeys, g * LANES:(g + 1) * LANES]
        qs = []
        for hh in range(GQA_GROUP):
            t = (g * GQA_GROUP + hh) // 2
            qt = q_ref[0, j * BLOCK:(j + 1) * BLOCK, t * LANES:(t + 1) * LANES]
            sel = low if hh % 2 == 0 else jnp.logical_not(low)
            qs.append(jnp.where(sel, qt, jnp.zeros_like(qt)))
        qg = jnp.concatenate(qs, axis=0)
        return lax.dot_general(qg, kg, (((1,), (1,)), ((), ())), preferred_element_type=F32)

    def softmax_pv(j, g, s):
        qblk = i * (ATTN_TILE // BLOCK) + j
        lo = jnp.where(qblk > 0, 0, BLOCK)
        hi = jnp.where(qblk < seq // BLOCK - 1, 0, -BLOCK)
        mask_prev = diff >= lo
        mask_next = diff <= hi
        vg = vcat[j * BLOCK:j * BLOCK + nkeys, g * LANES:(g + 1) * LANES]
        ps, rs = [], []
        for hh in range(GQA_GROUP):
            sk = sink_ref[g * GQA_GROUP + hh] * LOG2E
            sh = s[hh * BLOCK:(hh + 1) * BLOCK]
            sh = jnp.concatenate([jnp.where(mask_prev, sh[:, :BLOCK], -jnp.inf), sh[:, BLOCK:2 * BLOCK],
                                  jnp.where(mask_next, sh[:, 2 * BLOCK:], -jnp.inf)], axis=1)
            m = jnp.maximum(jnp.max(sh, axis=-1, keepdims=True), sk)
            p = jnp.exp2(sh - m)
            denom = jnp.sum(p, axis=-1, keepdims=True) + jnp.exp2(sk - m)
            ps.append(p.astype(BF16))
            rs.append(1.0 / denom)
        p = jnp.concatenate(ps, axis=0)
        o = jnp.dot(p, vg, preferred_element_type=F32)
        oh = [o[hh * BLOCK:(hh + 1) * BLOCK] * rs[hh] for hh in range(GQA_GROUP)]
        return [jnp.where(low, oh[2 * pair], oh[2 * pair + 1]) for pair in range(GQA_GROUP // 2)]

    pending = [scores(*u) for u in units[:ATTN_LOOKAHEAD]]
    tiles = []
    for n, (j, g) in enumerate(units):
        if n + ATTN_LOOKAHEAD < len(units):
            pending.append(scores(*units[n + ATTN_LOOKAHEAD]))
        tiles += softmax_pv(j, g, pending.pop(0))
        if g == N_KV_HEADS - 1:
            a = jnp.concatenate(tiles, axis=1)
            o_ref[0, j * BLOCK:(j + 1) * BLOCK, :] = _rms(a, gain, ATTN_WIDTH).astype(BF16)
            tiles = []


def _attention(sink, q, k2, v2, out_gain):
    nb, s, _ = q.shape
    t = ATTN_TILE
    r = t // BLOCK
    last = s // BLOCK - 1
    main = lambda w: pl.BlockSpec((1, t, w), lambda b, i, *_: (b, i, 0))
    prev = pl.BlockSpec((1, BLOCK, 2 * KV_WIDTH), lambda b, i, *_: (b, jnp.maximum(i * r - 1, 0), 0))
    nxt = pl.BlockSpec((1, BLOCK, 2 * KV_WIDTH), lambda b, i, *_: (b, jnp.minimum((i + 1) * r, last), 0))
    grid_spec = pltpu.PrefetchScalarGridSpec(
        num_scalar_prefetch=1,
        grid=(nb, s // t),
        in_specs=[main(ATTN_WIDTH), prev, main(2 * KV_WIDTH), nxt, prev, main(2 * KV_WIDTH), nxt,
                  pl.BlockSpec((1, ATTN_WIDTH), lambda b, i, *_: (0, 0))],
        out_specs=main(ATTN_WIDTH),
    )
    return pl.pallas_call(
        functools.partial(_attn_kernel, seq=s),
        grid_spec=grid_spec,
        out_shape=jax.ShapeDtypeStruct((nb, s, ATTN_WIDTH), BF16),
        compiler_params=_COMPILER_PARAMS,
        name="attention",
    )(sink, q, k2, k2, k2, v2, v2, v2, out_gain)


def _dft1_kernel(m1_ref, ab_ref, y_ref):
    m1 = m1_ref[...]
    for j in range(S1_GROUP):
        slabs = []
        for t in range(AB_SLABS):
            rows = ab_ref.at[0, t].reshape(DFT_N * S1_GROUP, LANES)
            slabs.append(rows[pl.ds(j, DFT_N, stride=S1_GROUP), :].astype(BF16))
        half = AB_SLABS // 2
        ab = jnp.concatenate([jnp.concatenate(slabs[:half], axis=1),
                              jnp.concatenate(slabs[half:], axis=1)], axis=0)
        y = jnp.dot(m1, ab, preferred_element_type=F32)
        y_ref[0, :, j * FOURIER_WIDTH:(j + 1) * FOURIER_WIDTH] = y.astype(BF16)


def _dft_stage1(m1, ab):
    nb = ab.shape[0]
    groups = DFT_N // S1_GROUP
    ab = ab.reshape(nb, AB_SLABS, DFT_N, groups, S1_GROUP, LANES)
    cols = S1_GROUP * FOURIER_WIDTH
    return pl.pallas_call(
        _dft1_kernel,
        grid=(nb, groups),
        in_specs=[_const_spec((2 * DFT_N, 2 * DFT_N)),
                  pl.BlockSpec((1, AB_SLABS, DFT_N, 1, S1_GROUP, LANES), lambda n, g: (n, 0, 0, g, 0, 0))],
        out_specs=pl.BlockSpec((1, 2 * DFT_N, cols), lambda n, g: (n, 0, g)),
        out_shape=jax.ShapeDtypeStruct((nb, 2 * DFT_N, DFT_N * FOURIER_WIDTH), BF16),
        compiler_params=_COMPILER_PARAMS,
        name="dft_stage1",
    )(m1, ab)


def _dft2_kernel(cn_ref, sn_ref, wc_ref, ws_ref, y_ref, wf_ref, g_ref, o_ref):
    cn = cn_ref[...]
    sn = sn_ref[...]
    wf = wf_ref[...]
    gain = g_ref[...]
    def seq_dft(j):
        wc = wc_ref[j:j + 1, :]
        ws = ws_ref[j:j + 1, :]
        gc = cn * wc - sn * ws
        gs = sn * wc + cn * ws
        g = jnp.concatenate([gc, gs], axis=1).astype(BF16)
        y = jnp.concatenate([y_ref[0, 0, j], y_ref[0, 1, j]], axis=0)
        return jnp.dot(g, y, preferred_element_type=F32).astype(BF16)

    xr = [seq_dft(j) for j in range(S2_K1)]
    fs = [jnp.dot(xr[j], wf, preferred_element_type=F32) for j in range(S2_K1)]
    for j in range(S2_K1):
        fn = _rms(fs[j], gain, FOURIER_WIDTH)
        for t in range(F_SLABS):
            rows = o_ref.at[0, t].reshape(DFT_N * S2_K1, LANES)
            rows[pl.ds(j, DFT_N, stride=S2_K1), :] = fn[:, t * LANES:(t + 1) * LANES]


def _dft_stage2(cn, sn, wc, ws, y, wf_bd, out_gain):
    nb = y.shape[0]
    y = y.reshape(nb, 2, DFT_N, DFT_N, FOURIER_WIDTH)
    groups = DFT_N // S2_K1
    tw = pl.BlockSpec((S2_K1, DFT_N), lambda n, k: (k, 0))
    out = pl.pallas_call(
        _dft2_kernel,
        grid=(nb, DFT_N // S2_K1),
        in_specs=[_const_spec((DFT_N, DFT_N)), _const_spec((DFT_N, DFT_N)), tw, tw,
                  pl.BlockSpec((1, 2, S2_K1, DFT_N, FOURIER_WIDTH), lambda n, k: (n, 0, k, 0, 0)),
                  _const_spec((FOURIER_WIDTH, FOURIER_WIDTH)), _const_spec((1, FOURIER_WIDTH))],
        out_specs=pl.BlockSpec((1, F_SLABS, DFT_N, 1, S2_K1, LANES), lambda n, k: (n, 0, 0, k, 0, 0)),
        out_shape=jax.ShapeDtypeStruct((nb, F_SLABS, DFT_N, groups, S2_K1, LANES), F32),
        compiler_params=_COMPILER_PARAMS,
        name="dft_stage2",
    )(cn, sn, wc, ws, y, wf_bd, out_gain)
    return out.reshape(nb, F_SLABS, DFT_N * DFT_N, LANES)


def _ffn_kernel(xm_ref, xp_ref, xn_ref, am_ref, ap_ref, an_ref, fm_ref, fp_ref, fn_ref, p_ref,
                wout_ref, gffn_ref, wup_ref, cw_ref, cb_ref, wdown_ref, wple_ref, gple_ref, wgate_ref,
                bgate_ref, o_ref):
    i = pl.program_id(1)
    nt = pl.num_programs(1)
    t = FFN_TILE
    ext = t + 2 * FFN_HALO
    x_ext = jnp.concatenate([xp_ref[0], xm_ref[0], xn_ref[0]], axis=0)
    a_ext = jnp.concatenate([ap_ref[0], am_ref[0], an_ref[0]], axis=0)
    f_ext = [jnp.concatenate([fp_ref[0, s], fm_ref[0, s], fn_ref[0, s]], axis=0).astype(BF16)
             for s in range(F_SLABS)]
    mixed = jnp.concatenate([a_ext] + f_ext, axis=1)
    x1 = x_ext + jnp.dot(mixed, wout_ref[...], preferred_element_type=F32)
    h = _rms(x1, gffn_ref[...], D_MODEL)
    row = lax.broadcasted_iota(jnp.int32, (ext, 1), 0)
    first_valid = jnp.where(i == 0, FFN_HALO, 0)
    end_valid = jnp.where(i == nt - 1, t + FFN_HALO, ext)
    h = jnp.where((row >= first_valid) & (row < end_valid), h, 0.0)
    trim = FFN_HALO - UP_HALO
    up_ext = t + 2 * UP_HALO
    h = h[trim:trim + up_ext].astype(BF16)

    acc = jnp.zeros((t, D_MODEL), F32)

    def up_proj(j):
        return tuple(jnp.dot(h, wup_ref[:, c0:c0 + FFN_CHUNK], preferred_element_type=F32)
                     for c0 in (j * FFN_CHUNK, D_FF + j * FFN_CHUNK))

    def conv(hu, c0):
        cw = cw_ref[:, c0:c0 + FFN_CHUNK]
        before = pltpu.roll(hu, 1, 0)[UP_HALO:UP_HALO + t]
        after = pltpu.roll(hu, up_ext - 1, 0)[UP_HALO:UP_HALO + t]
        return (before * cw[0:1] + hu[UP_HALO:UP_HALO + t] * cw[1:2] + after * cw[2:3]
                + cb_ref[:, c0:c0 + FFN_CHUNK])

    pending = [up_proj(j) for j in range(FFN_LOOKAHEAD)]
    for j in range(N_FFN_CHUNKS):
        if j + FFN_LOOKAHEAD < N_FFN_CHUNKS:
            pending.append(up_proj(j + FFN_LOOKAHEAD))
        hu_gate, hu_up = pending.pop(0)
        gate = conv(hu_gate, j * FFN_CHUNK)
        up = conv(hu_up, D_FF + j * FFN_CHUNK)
        act = (jax.nn.silu(gate) * up).astype(BF16)
        acc = acc + jnp.dot(act, wdown_ref[j * FFN_CHUNK:(j + 1) * FFN_CHUNK, :], preferred_element_type=F32)

    x2 = x1[FFN_HALO:FFN_HALO + t] + acc
    e = jnp.dot(p_ref[0].astype(BF16), wple_ref[...], preferred_element_type=F32)
    e = _rms(e, gple_ref[...], D_MODEL)
    gate = jax.nn.sigmoid(jnp.dot(x2.astype(BF16), wgate_ref[...], preferred_element_type=F32) + bgate_ref[...])
    o_ref[0] = x2 + gate * e


def _ffn(x, a, f, p, w_out, ffn_gain, w_up, conv_w, conv_b, w_down, w_ple, ple_gain, w_gate, b_gate):
    nb, s, _ = x.shape
    t = FFN_TILE
    r = t // FFN_HALO
    last = s // FFN_HALO - 1

    def specs(w):
        return [pl.BlockSpec((1, t, w), lambda b, i: (b, i, 0)),
                pl.BlockSpec((1, FFN_HALO, w), lambda b, i: (b, jnp.maximum(i * r - 1, 0), 0)),
                pl.BlockSpec((1, FFN_HALO, w), lambda b, i: (b, jnp.minimum((i + 1) * r, last), 0))]

    f_specs = [pl.BlockSpec((1, F_SLABS, t, LANES), lambda b, i: (b, 0, i, 0)),
               pl.BlockSpec((1, F_SLABS, FFN_HALO, LANES), lambda b, i: (b, 0, jnp.maximum(i * r - 1, 0), 0)),
               pl.BlockSpec((1, F_SLABS, FFN_HALO, LANES), lambda b, i: (b, 0, jnp.minimum((i + 1) * r, last), 0))]

    in_specs = (specs(D_MODEL) + specs(ATTN_WIDTH) + f_specs
                + [pl.BlockSpec((1, t, PLE_DIM), lambda b, i: (b, i, 0)),
                   _const_spec((D_MODEL, D_MODEL)), _const_spec((1, D_MODEL)),
                   _const_spec((D_MODEL, 2 * D_FF)), _const_spec((3, 2 * D_FF)), _const_spec((1, 2 * D_FF)),
                   _const_spec((D_FF, D_MODEL)), _const_spec((PLE_DIM, D_MODEL)), _const_spec((1, D_MODEL)),
                   _const_spec((D_MODEL, D_MODEL)), _const_spec((1, D_MODEL))])
    return pl.pallas_call(
        _ffn_kernel,
        grid=(nb, s // t),
        in_specs=in_specs,
        out_specs=pl.BlockSpec((1, t, D_MODEL), lambda b, i: (b, i, 0)),
        out_shape=jax.ShapeDtypeStruct((nb, s, D_MODEL), F32),
        compiler_params=_COMPILER_PARAMS,
        name="ffn",
    )(x, x, x, a, a, a, f, f, f, p, w_out, ffn_gain, w_up, conv_w, conv_b, w_down, w_ple, ple_gain,
      w_gate, b_gate)


def _dft_tables():
    n = np.arange(DFT_N)
    ang = 2.0 * np.pi * ((n[:, None] * n[None, :]) % DFT_N) / DFT_N
    cn, sn = np.cos(ang), np.sin(ang)
    m1 = np.block([[cn, -sn], [-sn, -cn]])
    seq = DFT_N * DFT_N
    tw = 2.0 * np.pi * ((n[:, None] * n[None, :]) % seq) / seq
    ortho = 1.0 / np.sqrt(seq * FOURIER_GROUP_DIM)
    c = np.arange(FOURIER_GROUP_DIM)
    ang_c = 2.0 * np.pi * ((c[:, None] * c[None, :]) % FOURIER_GROUP_DIM) / FOURIER_GROUP_DIM
    eye = np.eye(MXU_DIM // FOURIER_GROUP_DIM)
    cs = np.concatenate([np.kron(eye, np.cos(ang_c)), np.kron(eye, np.sin(ang_c))], axis=1)
    hsum = np.kron(np.eye(MXU_DIM // HEAD_DIM), np.ones((HEAD_DIM, HEAD_DIM)))
    f32 = lambda a: jnp.asarray(a, dtype=F32)
    return (f32(m1), f32(cn * ortho), f32(sn * ortho), f32(np.cos(tw)), f32(np.sin(tw)), f32(cs), f32(hsum))


def _rope_tables(seq):
    inv_freq = ROPE_THETA ** (-jnp.arange(0, ROT_DIM, 2, dtype=F32) / ROT_DIM)
    inv_head = jnp.concatenate([inv_freq, inv_freq, jnp.zeros((HEAD_DIM - ROT_DIM,), F32)])
    inv_lane = jnp.concatenate([inv_head] * (LANES // HEAD_DIM))
    ang = jnp.arange(seq, dtype=F32)[:, None] * inv_lane[None, :]
    cos, sin = jnp.cos(ang), jnp.sin(ang)
    d = (jnp.arange(LANES) % HEAD_DIM)[None, :]
    sin_a = jnp.where(d < ROT_HALF, -sin, 0.0)
    sin_b = jnp.where((d >= ROT_HALF) & (d < ROT_DIM), sin, 0.0)
    return cos, sin_a, sin_b


def _block_diag(w):
    g, c, e = w.shape
    eye = jnp.eye(g, dtype=w.dtype)
    return (eye[:, None, :, None] * w[:, :, None, :]).reshape(g * c, g * e)


def _layer(x, p, prm, tables):
    (attn_norm, w_in, q_norm, k_norm, sink, w_fourier, attn_out_norm, fourier_out_norm, w_out, ffn_norm,
     w_up, conv_w, conv_b, w_down, w_ple, ple_norm, w_ple_gate, b_ple_gate) = prm
    m1, cn, sn, wc, ws, cs_bd, hsum = tables
    seq = x.shape[1]
    assert seq == DFT_N * DFT_N
    cos_t, sin_a, sin_b = _rope_tables(seq)
    row = lambda v: v.reshape(1, -1)
    two = lambda v: jnp.concatenate([v] * (LANES // HEAD_DIM)).reshape(1, LANES)

    q, k2, v2, ab = _inproj(x, row(attn_norm), w_in.astype(BF16), two(q_norm), two(k_norm),
                            cos_t, sin_a, sin_b, hsum.astype(BF16), cs_bd.astype(BF16))
    attn = _attention(sink, q, k2, v2, row(attn_out_norm))
    y = _dft_stage1(m1.astype(BF16), ab)
    four = _dft_stage2(cn, sn, wc, ws, y, _block_diag(w_fourier).astype(BF16), row(fourier_out_norm))
    return _ffn(x, attn, four, p, w_out.astype(BF16), row(ffn_norm), w_up.astype(BF16), conv_w, row(conv_b),
                w_down.astype(BF16), w_ple.astype(BF16), row(ple_norm), w_ple_gate.astype(BF16),
                row(b_ple_gate))


def kernel(x_prompt, x_sample, p_prompt, p_sample, attn_norm, w_in, q_norm, k_norm, sink, w_fourier,
           attn_out_norm, fourier_out_norm, w_out, ffn_norm, w_up, conv_w, conv_b, w_down, w_ple, ple_norm,
           w_ple_gate, b_ple_gate):
    stacked = (attn_norm, w_in, q_norm, k_norm, sink, w_fourier, attn_out_norm, fourier_out_norm, w_out,
               ffn_norm, w_up, conv_w, conv_b, w_down, w_ple, ple_norm, w_ple_gate, b_ple_gate)
    tables = _dft_tables()
    y_prompt, y_sample = x_prompt, x_sample
    for i in range(attn_norm.shape[0]):
        prm = tuple(w[i] for w in stacked)
        y_prompt = _layer(y_prompt, p_prompt[i], prm, tables)
        y_sample = _layer(y_sample, p_sample[i], prm, tables)
    return (y_prompt, y_sample)
```

```python
import functools

import numpy as np
import jax
import jax.numpy as jnp
from jax import lax
from jax.experimental import pallas as pl
from jax.experimental.pallas import tpu as pltpu

F32 = jnp.float32
BF16 = jnp.bfloat16

D_MODEL = 1024
HEAD_DIM = 64
N_HEADS = 8
N_KV_HEADS = 2
GQA_GROUP = N_HEADS // N_KV_HEADS
ATTN_WIDTH = N_HEADS * HEAD_DIM
KV_WIDTH = N_KV_HEADS * HEAD_DIM
FOURIER_WIDTH = 512
FOURIER_GROUP_DIM = 64
N_FOURIER_GROUPS = FOURIER_WIDTH // FOURIER_GROUP_DIM
IN_WIDTH = ATTN_WIDTH + 2 * KV_WIDTH + FOURIER_WIDTH
BLOCK = 128
ROPE_THETA = 500000.0
ROT_DIM = HEAD_DIM // 4
ROT_HALF = ROT_DIM // 2
D_FF = 2816
PLE_DIM = 256
EPS = 1e-6
LOG2E = 1.4426950408889634

LANES = 128
MXU_DIM = 256
BF16_ROWS = 16
VMEM_LIMIT_BYTES = 56 * 1024 * 1024

DFT_N = 128

IN_TILE = 1024
ATTN_TILE = 1024
ATTN_LOOKAHEAD = 2
FFN_LOOKAHEAD = 2
F32_ROWS = 8
S1_GROUP = F32_ROWS
AB_SLABS = 2 * FOURIER_WIDTH // LANES
F_SLABS = FOURIER_WIDTH // LANES
S2_K1 = F32_ROWS
FFN_TILE = 256
FFN_SUBTILES = 2
FFN_HALO = BF16_ROWS
UP_HALO = F32_ROWS
FFN_CHUNK = 256
N_FFN_CHUNKS = D_FF // FFN_CHUNK


_COMPILER_PARAMS = pltpu.CompilerParams(
    dimension_semantics=("parallel", "parallel"),
    vmem_limit_bytes=VMEM_LIMIT_BYTES,
)


def _rms(x, gain, n):
    ms = jnp.sum(x * x, axis=-1, keepdims=True) * (1.0 / n)
    return x * lax.rsqrt(ms + EPS) * gain


def _const_spec(shape):
    zeros = (0,) * len(shape)
    return pl.BlockSpec(shape, lambda *_: zeros, pipeline_mode=pl.Buffered(1))


def _inproj_kernel(x_ref, g_ref, win_ref, qg_ref, kg_ref, cos_ref, sina_ref, sinb_ref, hsum_ref, cs_ref,
                   q_ref, k2_ref, v2_ref, ab_ref):
    x = x_ref[0]
    h = _rms(x, g_ref[...], D_MODEL).astype(BF16)
    z = jnp.dot(h, win_ref[...], preferred_element_type=F32)
    cos_t = cos_ref[...]
    sin_a = sina_ref[...]
    sin_b = sinb_ref[...]

    def norm_rope(t, gain):
        w = t.shape[1]
        wide = lambda a: jnp.concatenate([a] * (w // LANES), axis=1)
        ssq = jnp.dot((t * t).astype(BF16), hsum_ref[:w, :w], preferred_element_type=F32)
        tn = t * lax.rsqrt(ssq * (1.0 / HEAD_DIM) + EPS) * wide(gain)
        up = pltpu.roll(tn, w - ROT_HALF, 1)
        dn = pltpu.roll(tn, ROT_HALF, 1)
        return tn * wide(cos_t) + up * wide(sin_a) + dn * wide(sin_b)

    scale = HEAD_DIM ** -0.5 * LOG2E
    for t in range(ATTN_WIDTH // MXU_DIM):
        qt = norm_rope(z[:, t * MXU_DIM:(t + 1) * MXU_DIM], qg_ref[...])
        q_ref[0, :, t * MXU_DIM:(t + 1) * MXU_DIM] = (qt * scale).astype(BF16)

    lane = lax.broadcasted_iota(jnp.int32, (x.shape[0], LANES), 1)
    low = lane < HEAD_DIM

    def dup_heads(t):
        r = pltpu.roll(t, HEAD_DIM, 1)
        return jnp.concatenate([jnp.where(low, t, r), jnp.where(low, r, t)], axis=1)

    k = norm_rope(z[:, ATTN_WIDTH:ATTN_WIDTH + KV_WIDTH], kg_ref[...])
    k2_ref[0] = dup_heads(k).astype(BF16)
    v = z[:, ATTN_WIDTH + KV_WIDTH:ATTN_WIDTH + 2 * KV_WIDTH]
    v2_ref[0] = dup_heads(v).astype(BF16)

    u = z[:, ATTN_WIDTH + 2 * KV_WIDTH:].astype(BF16)
    halves = [jnp.dot(u[:, c * MXU_DIM:(c + 1) * MXU_DIM], cs_ref[...], preferred_element_type=F32)
              for c in range(FOURIER_WIDTH // MXU_DIM)]
    per_half = MXU_DIM // LANES
    for part in range(2):
        for c, ab in enumerate(halves):
            for t in range(per_half):
                lo = part * MXU_DIM + t * LANES
                ab_ref[0, part * F_SLABS + c * per_half + t] = ab[:, lo:lo + LANES]


def _inproj(x, attn_norm, w_in, q_gain, k_gain, cos_t, sin_a, sin_b, hsum, cs_bd):
    nb, s, _ = x.shape
    t = IN_TILE
    tok = lambda w: pl.BlockSpec((1, t, w), lambda b, i: (b, i, 0))
    pos = pl.BlockSpec((t, LANES), lambda b, i: (i, 0))
    out_shape = (
        jax.ShapeDtypeStruct((nb, s, ATTN_WIDTH), BF16),
        jax.ShapeDtypeStruct((nb, s, 2 * KV_WIDTH), BF16),
        jax.ShapeDtypeStruct((nb, s, 2 * KV_WIDTH), BF16),
        jax.ShapeDtypeStruct((nb, AB_SLABS, s, LANES), F32),
    )
    return pl.pallas_call(
        _inproj_kernel,
        grid=(nb, s // t),
        in_specs=[tok(D_MODEL), _const_spec((1, D_MODEL)), _const_spec((D_MODEL, IN_WIDTH)),
                  _const_spec((1, LANES)), _const_spec((1, LANES)), pos, pos, pos,
                  _const_spec((MXU_DIM, MXU_DIM)), _const_spec((MXU_DIM, 2 * MXU_DIM))],
        out_specs=(tok(ATTN_WIDTH), tok(2 * KV_WIDTH), tok(2 * KV_WIDTH),
                   pl.BlockSpec((1, AB_SLABS, t, LANES), lambda b, i: (b, 0, i, 0))),
        out_shape=out_shape,
        compiler_params=_COMPILER_PARAMS,
        name="inproj",
    )(x, attn_norm, w_in, q_gain, k_gain, cos_t, sin_a, sin_b, hsum, cs_bd)


def _attn_kernel(sink_ref, q_ref, kp_ref, km_ref, kn_ref, vp_ref, vm_ref, vn_ref, g_ref, o_ref, *, seq):
    i = pl.program_id(1)
    kcat = jnp.concatenate([kp_ref[0], km_ref[0], kn_ref[0]], axis=0)
    vcat = jnp.concatenate([vp_ref[0], vm_ref[0], vn_ref[0]], axis=0)
    nkeys = 3 * BLOCK
    diff = (lax.broadcasted_iota(jnp.int32, (BLOCK, BLOCK), 1)
            - lax.broadcasted_iota(jnp.int32, (BLOCK, BLOCK), 0))
    lane = lax.broadcasted_iota(jnp.int32, (BLOCK, LANES), 1)
    low = lane < HEAD_DIM
    gain = g_ref[...]
    units = [(j, g) for j in range(ATTN_TILE // BLOCK) for g in range(N_KV_HEADS)]

    def scores(j, g):
        kg = kcat[j * BLOCK:j * BLOCK + nkeys, g * LANES:(g + 1) * LANES]
        qs = []
        for hh in range(GQA_GROUP):
            t = (g * GQA_GROUP + hh) // 2
            qt = q_ref[0, j * BLOCK:(j + 1) * BLOCK, t * LANES:(t + 1) * LANES]
            sel = low if hh % 2 == 0 else jnp.logical_not(low)
            qs.append(jnp.where(sel, qt, jnp.zeros_like(qt)))
        qg = jnp.concatenate(qs, axis=0)
        return lax.dot_general(qg, kg, (((1,), (1,)), ((), ())), preferred_element_type=F32)

    def softmax_pv(j, g, s):
        qblk = i * (ATTN_TILE // BLOCK) + j
        lo = jnp.where(qblk > 0, 0, BLOCK)
        hi = jnp.where(qblk < seq // BLOCK - 1, 0, -BLOCK)
        mask_prev = diff >= lo
        mask_next = diff <= hi
        vg = vcat[j * BLOCK:j * BLOCK + nkeys, g * LANES:(g + 1) * LANES]
        ps, rs = [], []
        for hh in range(GQA_GROUP):
            sk = sink_ref[g * GQA_GROUP + hh] * LOG2E
            sh = s[hh * BLOCK:(hh + 1) * BLOCK]
            sh = jnp.concatenate([jnp.where(mask_prev, sh[:, :BLOCK], -jnp.inf), sh[:, BLOCK:2 * BLOCK],
                                  jnp.where(mask_next, sh[:, 2 * BLOCK:], -jnp.inf)], axis=1)
            m = jnp.maximum(jnp.max(sh, axis=-1, keepdims=True), sk)
            p = jnp.exp2(sh - m)
            denom = jnp.sum(p, axis=-1, keepdims=True) + jnp.exp2(sk - m)
            ps.append(p.astype(BF16))
            rs.append(1.0 / denom)
        p = jnp.concatenate(ps, axis=0)
        o = jnp.dot(p, vg, preferred_element_type=F32)
        oh = [o[hh * BLOCK:(hh + 1) * BLOCK] * rs[hh] for hh in range(GQA_GROUP)]
        return [jnp.where(low, oh[2 * pair], oh[2 * pair + 1]) for pair in range(GQA_GROUP // 2)]

    pending = [scores(*u) for u in units[:ATTN_LOOKAHEAD]]
    tiles = []
    for n, (j, g) in enumerate(units):
        if n + ATTN_LOOKAHEAD < len(units):
            pending.append(scores(*units[n + ATTN_LOOKAHEAD]))
        tiles += softmax_pv(j, g, pending.pop(0))
        if g == N_KV_HEADS - 1:
            a = jnp.concatenate(tiles, axis=1)
            o_ref[0, j * BLOCK:(j + 1) * BLOCK, :] = _rms(a, gain, ATTN_WIDTH).astype(BF16)
            tiles = []


def _attention(sink, q, k2, v2, out_gain):
    nb, s, _ = q.shape
    t = ATTN_TILE
    r = t // BLOCK
    last = s // BLOCK - 1
    main = lambda w: pl.BlockSpec((1, t, w), lambda b, i, *_: (b, i, 0))
    prev = pl.BlockSpec((1, BLOCK, 2 * KV_WIDTH), lambda b, i, *_: (b, jnp.maximum(i * r - 1, 0), 0))
    nxt = pl.BlockSpec((1, BLOCK, 2 * KV_WIDTH), lambda b, i, *_: (b, jnp.minimum((i + 1) * r, last), 0))
    grid_spec = pltpu.PrefetchScalarGridSpec(
        num_scalar_prefetch=1,
        grid=(nb, s // t),
        in_specs=[main(ATTN_WIDTH), prev, main(2 * KV_WIDTH), nxt, prev, main(2 * KV_WIDTH), nxt,
                  pl.BlockSpec((1, ATTN_WIDTH), lambda b, i, *_: (0, 0))],
        out_specs=main(ATTN_WIDTH),
    )
    return pl.pallas_call(
        functools.partial(_attn_kernel, seq=s),
        grid_spec=grid_spec,
        out_shape=jax.ShapeDtypeStruct((nb, s, ATTN_WIDTH), BF16),
        compiler_params=_COMPILER_PARAMS,
        name="attention",
    )(sink, q, k2, k2, k2, v2, v2, v2, out_gain)


def _dft1_kernel(m1_ref, ab_ref, y_ref):
    m1 = m1_ref[...]
    for j in range(S1_GROUP):
        slabs = []
        for t in range(AB_SLABS):
            rows = ab_ref.at[0, t].reshape(DFT_N * S1_GROUP, LANES)
            slabs.append(rows[pl.ds(j, DFT_N, stride=S1_GROUP), :].astype(BF16))
        half = AB_SLABS // 2
        ab = jnp.concatenate([jnp.concatenate(slabs[:half], axis=1),
                              jnp.concatenate(slabs[half:], axis=1)], axis=0)
        y = jnp.dot(m1, ab, preferred_element_type=F32)
        y_ref[0, :, j * FOURIER_WIDTH:(j + 1) * FOURIER_WIDTH] = y.astype(BF16)


def _dft_stage1(m1, ab):
    nb = ab.shape[0]
    groups = DFT_N // S1_GROUP
    ab = ab.reshape(nb, AB_SLABS, DFT_N, groups, S1_GROUP, LANES)
    cols = S1_GROUP * FOURIER_WIDTH
    return pl.pallas_call(
        _dft1_kernel,
        grid=(nb, groups),
        in_specs=[_const_spec((2 * DFT_N, 2 * DFT_N)),
                  pl.BlockSpec((1, AB_SLABS, DFT_N, 1, S1_GROUP, LANES), lambda n, g: (n, 0, 0, g, 0, 0))],
        out_specs=pl.BlockSpec((1, 2 * DFT_N, cols), lambda n, g: (n, 0, g)),
        out_shape=jax.ShapeDtypeStruct((nb, 2 * DFT_N, DFT_N * FOURIER_WIDTH), BF16),
        compiler_params=_COMPILER_PARAMS,
        name="dft_stage1",
    )(m1, ab)


def _dft2_kernel(cn_ref, sn_ref, wc_ref, ws_ref, y_ref, wf_ref, g_ref, o_ref):
    cn = cn_ref[...]
    sn = sn_ref[...]
    wf = wf_ref[...]
    gain = g_ref[...]
    def seq_dft(j):
        wc = wc_ref[j:j + 1, :]
        ws = ws_ref[j:j + 1, :]
        gc = cn * wc - sn * ws
        gs = sn * wc + cn * ws
        g = jnp.concatenate([gc, gs], axis=1).astype(BF16)
        y = jnp.concatenate([y_ref[0, 0, j], y_ref[0, 1, j]], axis=0)
        return jnp.dot(g, y, preferred_element_type=F32).astype(BF16)

    xr = [seq_dft(j) for j in range(S2_K1)]
    fs = [jnp.dot(xr[j], wf, preferred_element_type=F32) for j in range(S2_K1)]
    for j in range(S2_K1):
        fn = _rms(fs[j], gain, FOURIER_WIDTH)
        for t in range(F_SLABS):
            rows = o_ref.at[0, t].reshape(DFT_N * S2_K1, LANES)
            rows[pl.ds(j, DFT_N, stride=S2_K1), :] = fn[:, t * LANES:(t + 1) * LANES]


def _dft_stage2(cn, sn, wc, ws, y, wf_bd, out_gain):
    nb = y.shape[0]
    y = y.reshape(nb, 2, DFT_N, DFT_N, FOURIER_WIDTH)
    groups = DFT_N // S2_K1
    tw = pl.BlockSpec((S2_K1, DFT_N), lambda n, k: (k, 0))
    out = pl.pallas_call(
        _dft2_kernel,
        grid=(nb, DFT_N // S2_K1),
        in_specs=[_const_spec((DFT_N, DFT_N)), _const_spec((DFT_N, DFT_N)), tw, tw,
                  pl.BlockSpec((1, 2, S2_K1, DFT_N, FOURIER_WIDTH), lambda n, k: (n, 0, k, 0, 0)),
                  _const_spec((FOURIER_WIDTH, FOURIER_WIDTH)), _const_spec((1, FOURIER_WIDTH))],
        out_specs=pl.BlockSpec((1, F_SLABS, DFT_N, 1, S2_K1, LANES), lambda n, k: (n, 0, 0, k, 0, 0)),
        out_shape=jax.ShapeDtypeStruct((nb, F_SLABS, DFT_N, groups, S2_K1, LANES), F32),
        compiler_params=_COMPILER_PARAMS,
        name="dft_stage2",
    )(cn, sn, wc, ws, y, wf_bd, out_gain)
    return out.reshape(nb, F_SLABS, DFT_N * DFT_N, LANES)


def _ffn_kernel(xm_ref, xp_ref, xn_ref, am_ref, ap_ref, an_ref, fm_ref, fp_ref, fn_ref, p_ref,
                wout_ref, gffn_ref, wup_ref, cw_ref, cb_ref, wdown_ref, wple_ref, gple_ref, wgate_ref,
                bgate_ref, o_ref):
    i = pl.program_id(1)
    nt = pl.num_programs(1)
    t = FFN_TILE
    ext = t + 2 * FFN_HALO
    trim = FFN_HALO - UP_HALO
    up_ext = t + 2 * UP_HALO
    x_all = jnp.concatenate([xp_ref[0], xm_ref[0], xn_ref[0]], axis=0)
    a_all = jnp.concatenate([ap_ref[0], am_ref[0], an_ref[0]], axis=0)
    f_all = [jnp.concatenate([fp_ref[0, s], fm_ref[0, s], fn_ref[0, s]], axis=0).astype(BF16)
             for s in range(F_SLABS)]
    mixed_all = jnp.concatenate([a_all] + f_all, axis=1)
    row = lax.broadcasted_iota(jnp.int32, (ext, 1), 0)

    def prologue(s):
        r0 = s * t
        x1 = x_all[r0:r0 + ext] + jnp.dot(mixed_all[r0:r0 + ext], wout_ref[...], preferred_element_type=F32)
        e = jnp.dot(p_ref[0, r0:r0 + t].astype(BF16), wple_ref[...], preferred_element_type=F32)
        e = _rms(e, gple_ref[...], D_MODEL)
        h = _rms(x1, gffn_ref[...], D_MODEL)
        first_valid = jnp.where(i == 0, FFN_HALO, 0) if s == 0 else 0
        end_valid = jnp.where(i == nt - 1, t + FFN_HALO, ext) if s == FFN_SUBTILES - 1 else ext
        h = jnp.where((row >= first_valid) & (row < end_valid), h, 0.0)
        return x1[FFN_HALO:FFN_HALO + t], e, h[trim:trim + up_ext].astype(BF16)

    def up_proj(h, j):
        return tuple(jnp.dot(h, wup_ref[:, c0:c0 + FFN_CHUNK], preferred_element_type=F32)
                     for c0 in (j * FFN_CHUNK, D_FF + j * FFN_CHUNK))

    def conv(hu, c0):
        cw = cw_ref[:, c0:c0 + FFN_CHUNK]
        before = pltpu.roll(hu, 1, 0)[UP_HALO:UP_HALO + t]
        after = pltpu.roll(hu, up_ext - 1, 0)[UP_HALO:UP_HALO + t]
        return (before * cw[0:1] + hu[UP_HALO:UP_HALO + t] * cw[1:2] + after * cw[2:3]
                + cb_ref[:, c0:c0 + FFN_CHUNK])

    def epilogue(s, x1, e, acc):
        x2 = x1 + acc
        gate = jax.nn.sigmoid(jnp.dot(x2.astype(BF16), wgate_ref[...], preferred_element_type=F32)
                              + bgate_ref[...])
        o_ref[0, s * t:(s + 1) * t, :] = x2 + gate * e

    units = [(s, j) for s in range(FFN_SUBTILES) for j in range(N_FFN_CHUNKS)]
    state = {}
    pending = []

    def issue(n):
        s, j = units[n]
        if j == 0:
            state[s] = list(prologue(s)) + [jnp.zeros((t, D_MODEL), F32)]
        pending.append(up_proj(state[s][2], j))

    for n in range(FFN_LOOKAHEAD):
        issue(n)
    for n, (s, j) in enumerate(units):
        if n + FFN_LOOKAHEAD < len(units):
            issue(n + FFN_LOOKAHEAD)
        hu_gate, hu_up = pending.pop(0)
        gate = conv(hu_gate, j * FFN_CHUNK)
        up = conv(hu_up, D_FF + j * FFN_CHUNK)
        act = (jax.nn.silu(gate) * up).astype(BF16)
        state[s][3] = state[s][3] + jnp.dot(act, wdown_ref[j * FFN_CHUNK:(j + 1) * FFN_CHUNK, :],
                                            preferred_element_type=F32)
        if j == N_FFN_CHUNKS - 1:
            x1, e, _, acc = state.pop(s)
            epilogue(s, x1, e, acc)


def _ffn(x, a, f, p, w_out, ffn_gain, w_up, conv_w, conv_b, w_down, w_ple, ple_gain, w_gate, b_gate):
    nb, s, _ = x.shape
    t = FFN_SUBTILES * FFN_TILE
    r = t // FFN_HALO
    last = s // FFN_HALO - 1

    def specs(w):
        return [pl.BlockSpec((1, t, w), lambda b, i: (b, i, 0)),
                pl.BlockSpec((1, FFN_HALO, w), lambda b, i: (b, jnp.maximum(i * r - 1, 0), 0)),
                pl.BlockSpec((1, FFN_HALO, w), lambda b, i: (b, jnp.minimum((i + 1) * r, last), 0))]

    f_specs = [pl.BlockSpec((1, F_SLABS, t, LANES), lambda b, i: (b, 0, i, 0)),
               pl.BlockSpec((1, F_SLABS, FFN_HALO, LANES), lambda b, i: (b, 0, jnp.maximum(i * r - 1, 0), 0)),
               pl.BlockSpec((1, F_SLABS, FFN_HALO, LANES), lambda b, i: (b, 0, jnp.minimum((i + 1) * r, last), 0))]

    in_specs = (specs(D_MODEL) + specs(ATTN_WIDTH) + f_specs
                + [pl.BlockSpec((1, t, PLE_DIM), lambda b, i: (b, i, 0)),
                   _const_spec((D_MODEL, D_MODEL)), _const_spec((1, D_MODEL)),
                   _const_spec((D_MODEL, 2 * D_FF)), _const_spec((3, 2 * D_FF)), _const_spec((1, 2 * D_FF)),
                   _const_spec((D_FF, D_MODEL)), _const_spec((PLE_DIM, D_MODEL)), _const_spec((1, D_MODEL)),
                   _const_spec((D_MODEL, D_MODEL)), _const_spec((1, D_MODEL))])
    return pl.pallas_call(
        _ffn_kernel,
        grid=(nb, s // t),
        in_specs=in_specs,
        out_specs=pl.BlockSpec((1, t, D_MODEL), lambda b, i: (b, i, 0)),
        out_shape=jax.ShapeDtypeStruct((nb, s, D_MODEL), F32),
        compiler_params=_COMPILER_PARAMS,
        name="ffn",
    )(x, x, x, a, a, a, f, f, f, p, w_out, ffn_gain, w_up, conv_w, conv_b, w_down, w_ple, ple_gain,
      w_gate, b_gate)


def _dft_tables():
    n = np.arange(DFT_N)
    ang = 2.0 * np.pi * ((n[:, None] * n[None, :]) % DFT_N) / DFT_N
    cn, sn = np.cos(ang), np.sin(ang)
    m1 = np.block([[cn, -sn], [-sn, -cn]])
    seq = DFT_N * DFT_N
    tw = 2.0 * np.pi * ((n[:, None] * n[None, :]) % seq) / seq
    ortho = 1.0 / np.sqrt(seq * FOURIER_GROUP_DIM)
    c = np.arange(FOURIER_GROUP_DIM)
    ang_c = 2.0 * np.pi * ((c[:, None] * c[None, :]) % FOURIER_GROUP_DIM) / FOURIER_GROUP_DIM
    eye = np.eye(MXU_DIM // FOURIER_GROUP_DIM)
    cs = np.concatenate([np.kron(eye, np.cos(ang_c)), np.kron(eye, np.sin(ang_c))], axis=1)
    hsum = np.kron(np.eye(MXU_DIM // HEAD_DIM), np.ones((HEAD_DIM, HEAD_DIM)))
    f32 = lambda a: jnp.asarray(a, dtype=F32)
    return (f32(m1), f32(cn * ortho), f32(sn * ortho), f32(np.cos(tw)), f32(np.sin(tw)), f32(cs), f32(hsum))


def _rope_tables(seq):
    inv_freq = ROPE_THETA ** (-jnp.arange(0, ROT_DIM, 2, dtype=F32) / ROT_DIM)
    inv_head = jnp.concatenate([inv_freq, inv_freq, jnp.zeros((HEAD_DIM - ROT_DIM,), F32)])
    inv_lane = jnp.concatenate([inv_head] * (LANES // HEAD_DIM))
    ang = jnp.arange(seq, dtype=F32)[:, None] * inv_lane[None, :]
    cos, sin = jnp.cos(ang), jnp.sin(ang)
    d = (jnp.arange(LANES) % HEAD_DIM)[None, :]
    sin_a = jnp.where(d < ROT_HALF, -sin, 0.0)
    sin_b = jnp.where((d >= ROT_HALF) & (d < ROT_DIM), sin, 0.0)
    return cos, sin_a, sin_b


def _block_diag(w):
    g, c, e = w.shape
    eye = jnp.eye(g, dtype=w.dtype)
    return (eye[:, None, :, None] * w[:, :, None, :]).reshape(g * c, g * e)


def _layer(x, p, prm, tables):
    (attn_norm, w_in, q_norm, k_norm, sink, w_fourier, attn_out_norm, fourier_out_norm, w_out, ffn_norm,
     w_up, conv_w, conv_b, w_down, w_ple, ple_norm, w_ple_gate, b_ple_gate) = prm
    m1, cn, sn, wc, ws, cs_bd, hsum = tables
    seq = x.shape[1]
    assert seq == DFT_N * DFT_N
    cos_t, sin_a, sin_b = _rope_tables(seq)
    row = lambda v: v.reshape(1, -1)
    two = lambda v: jnp.concatenate([v] * (LANES // HEAD_DIM)).reshape(1, LANES)

    q, k2, v2, ab = _inproj(x, row(attn_norm), w_in.astype(BF16), two(q_norm), two(k_norm),
                            cos_t, sin_a, sin_b, hsum.astype(BF16), cs_bd.astype(BF16))
    attn = _attention(sink, q, k2, v2, row(attn_out_norm))
    y = _dft_stage1(m1.astype(BF16), ab)
    four = _dft_stage2(cn, sn, wc, ws, y, _block_diag(w_fourier).astype(BF16), row(fourier_out_norm))
    return _ffn(x, attn, four, p, w_out.astype(BF16), row(ffn_norm), w_up.astype(BF16), conv_w, row(conv_b),
                w_down.astype(BF16), w_ple.astype(BF16), row(ple_norm), w_ple_gate.astype(BF16),
                row(b_ple_gate))


def kernel(x_prompt, x_sample, p_prompt, p_sample, attn_norm, w_in, q_norm, k_norm, sink, w_fourier,
           attn_out_norm, fourier_out_norm, w_out, ffn_norm, w_up, conv_w, conv_b, w_down, w_ple, ple_norm,
           w_ple_gate, b_ple_gate):
    stacked = (attn_norm, w_in, q_norm, k_norm, sink, w_fourier, attn_out_norm, fourier_out_norm, w_out,
               ffn_norm, w_up, conv_w, conv_b, w_down, w_ple, ple_norm, w_ple_gate, b_ple_gate)
    tables = _dft_tables()
    y_prompt, y_sample = x_prompt, x_sample
    for i in range(attn_norm.shape[0]):
        prm = tuple(w[i] for w in stacked)
        y_prompt = _layer(y_prompt, p_prompt[i], prm, tables)
        y_sample = _layer(y_sample, p_sample[i], prm, tables)
    return (y_prompt, y_sample)
```

```python
import functools

import numpy as np
import jax
import jax.numpy as jnp
from jax import lax
from jax.experimental import pallas as pl
from jax.experimental.pallas import tpu as pltpu

F32 = jnp.float32
BF16 = jnp.bfloat16

D_MODEL = 1024
HEAD_DIM = 64
N_HEADS = 8
N_KV_HEADS = 2
GQA_GROUP = N_HEADS // N_KV_HEADS
ATTN_WIDTH = N_HEADS * HEAD_DIM
KV_WIDTH = N_KV_HEADS * HEAD_DIM
FOURIER_WIDTH = 512
FOURIER_GROUP_DIM = 64
N_FOURIER_GROUPS = FOURIER_WIDTH // FOURIER_GROUP_DIM
IN_WIDTH = ATTN_WIDTH + 2 * KV_WIDTH + FOURIER_WIDTH
BLOCK = 128
ROPE_THETA = 500000.0
ROT_DIM = HEAD_DIM // 4
ROT_HALF = ROT_DIM // 2
D_FF = 2816
PLE_DIM = 256
EPS = 1e-6
LOG2E = 1.4426950408889634

LANES = 128
MXU_DIM = 256
BF16_ROWS = 16
VMEM_LIMIT_BYTES = 56 * 1024 * 1024

DFT_N = 128

IN_TILE = 1024
IN_SUB = 128
ATTN_TILE = 1024
ATTN_LOOKAHEAD = 2
FFN_LOOKAHEAD = 2
F32_ROWS = 8
S1_GROUP = F32_ROWS
F_SLABS = FOURIER_WIDTH // LANES
S2_K1 = F32_ROWS
FFN_TILE = 256
FFN_SUBTILES = 2
FFN_HALO = BF16_ROWS
UP_HALO = F32_ROWS
FFN_CHUNK = 256
N_FFN_CHUNKS = D_FF // FFN_CHUNK


_COMPILER_PARAMS = pltpu.CompilerParams(
    dimension_semantics=("parallel", "parallel"),
    vmem_limit_bytes=VMEM_LIMIT_BYTES,
)


def _rms(x, gain, n):
    ms = jnp.sum(x * x, axis=-1, keepdims=True) * (1.0 / n)
    return x * lax.rsqrt(ms + EPS) * gain


def _const_spec(shape):
    zeros = (0,) * len(shape)
    return pl.BlockSpec(shape, lambda *_: zeros, pipeline_mode=pl.Buffered(1))


def _inproj_kernel(x_ref, g_ref, win_ref, qg_ref, kg_ref, cos_ref, sin_ref, hsum_ref, q_ref, k_ref, v_ref, u_ref):
    d = lax.broadcasted_iota(jnp.int32, (IN_SUB, LANES), 1) % HEAD_DIM
    scale = HEAD_DIM ** -0.5 * LOG2E

    def project(r0):
        h = _rms(x_ref[0, r0:r0 + IN_SUB], g_ref[...], D_MODEL).astype(BF16)
        return jnp.dot(h, win_ref[...], preferred_element_type=F32)

    def finish(r0, z):
        cos_t = cos_ref[r0:r0 + IN_SUB]
        sin_t = sin_ref[r0:r0 + IN_SUB]
        sin_lo = jnp.where(d < ROT_HALF, -sin_t, 0.0)
        sin_hi = jnp.where(d >= ROT_HALF, sin_t, 0.0)

        def norm_rope(t, gain):
            w = t.shape[1]
            wide = lambda a: jnp.concatenate([a] * (w // LANES), axis=1)
            ssq = jnp.dot((t * t).astype(BF16), hsum_ref[:w, :w], preferred_element_type=F32)
            tn = t * lax.rsqrt(ssq * (1.0 / HEAD_DIM) + EPS) * wide(gain)
            up = pltpu.roll(tn, w - ROT_HALF, 1)
            dn = pltpu.roll(tn, ROT_HALF, 1)
            return tn * wide(cos_t) + up * wide(sin_lo) + dn * wide(sin_hi)

        for t in range(ATTN_WIDTH // MXU_DIM):
            qt = norm_rope(z[:, t * MXU_DIM:(t + 1) * MXU_DIM], qg_ref[...])
            q_ref[0, r0:r0 + IN_SUB, t * MXU_DIM:(t + 1) * MXU_DIM] = (qt * scale).astype(BF16)
        k_ref[0, r0:r0 + IN_SUB] = norm_rope(z[:, ATTN_WIDTH:ATTN_WIDTH + KV_WIDTH], kg_ref[...]).astype(BF16)
        v_ref[0, r0:r0 + IN_SUB] = z[:, ATTN_WIDTH + KV_WIDTH:ATTN_WIDTH + 2 * KV_WIDTH].astype(BF16)
        for t in range(F_SLABS):
            lo = ATTN_WIDTH + 2 * KV_WIDTH + t * LANES
            u_ref[0, t, r0:r0 + IN_SUB] = z[:, lo:lo + LANES]

    starts = list(range(0, IN_TILE, IN_SUB))
    z = project(starts[0])
    for n, r0 in enumerate(starts):
        z_next = project(starts[n + 1]) if n + 1 < len(starts) else None
        finish(r0, z)
        z = z_next


def _inproj(x, attn_norm, w_in, q_gain, k_gain, cos_t, sin_t, hsum):
    nb, s, _ = x.shape
    t = IN_TILE
    tok = lambda w: pl.BlockSpec((1, t, w), lambda b, i: (b, i, 0))
    pos = pl.BlockSpec((t, LANES), lambda b, i: (i, 0))
    out_shape = (
        jax.ShapeDtypeStruct((nb, s, ATTN_WIDTH), BF16),
        jax.ShapeDtypeStruct((nb, s, KV_WIDTH), BF16),
        jax.ShapeDtypeStruct((nb, s, KV_WIDTH), BF16),
        jax.ShapeDtypeStruct((nb, F_SLABS, s, LANES), F32),
    )
    return pl.pallas_call(
        _inproj_kernel,
        grid=(nb, s // t),
        in_specs=[tok(D_MODEL), _const_spec((1, D_MODEL)), _const_spec((D_MODEL, IN_WIDTH)),
                  _const_spec((1, LANES)), _const_spec((1, LANES)), pos, pos, _const_spec((MXU_DIM, MXU_DIM))],
        out_specs=(tok(ATTN_WIDTH), tok(KV_WIDTH), tok(KV_WIDTH),
                   pl.BlockSpec((1, F_SLABS, t, LANES), lambda b, i: (b, 0, i, 0))),
        out_shape=out_shape,
        compiler_params=_COMPILER_PARAMS,
        name="inproj",
    )(x, attn_norm, w_in, q_gain, k_gain, cos_t, sin_t, hsum)


def _attn_kernel(sink_ref, q_ref, kp_ref, km_ref, kn_ref, vp_ref, vm_ref, vn_ref, g_ref, o_ref, *, seq):
    i = pl.program_id(1)
    kcat = jnp.concatenate([kp_ref[0], km_ref[0], kn_ref[0]], axis=0)
    vcat = jnp.concatenate([vp_ref[0], vm_ref[0], vn_ref[0]], axis=0)
    nkeys = 3 * BLOCK
    diff = (lax.broadcasted_iota(jnp.int32, (BLOCK, BLOCK), 1)
            - lax.broadcasted_iota(jnp.int32, (BLOCK, BLOCK), 0))
    lane = lax.broadcasted_iota(jnp.int32, (BLOCK, LANES), 1)
    low = lane < HEAD_DIM
    gain = g_ref[...]
    units = [(j, g) for j in range(ATTN_TILE // BLOCK) for g in range(N_KV_HEADS)]

    def scores(j, g):
        kw = kcat[j * BLOCK:j * BLOCK + nkeys]
        sel = low if g == 0 else jnp.logical_not(low)
        qs = []
        for hh in range(GQA_GROUP):
            qt = q_ref[0, j * BLOCK:(j + 1) * BLOCK, hh * LANES:(hh + 1) * LANES]
            qs.append(jnp.where(sel, qt, jnp.zeros_like(qt)))
        qg = jnp.concatenate(qs, axis=0)
        return lax.dot_general(qg, kw, (((1,), (1,)), ((), ())), preferred_element_type=F32)

    def softmax_pv(j, g, s):
        qblk = i * (ATTN_TILE // BLOCK) + j
        lo = jnp.where(qblk > 0, 0, BLOCK)
        hi = jnp.where(qblk < seq // BLOCK - 1, 0, -BLOCK)
        mask_prev = diff >= lo
        mask_next = diff <= hi
        vw = vcat[j * BLOCK:j * BLOCK + nkeys]
        ps, rs = [], []
        for hh in range(GQA_GROUP):
            sk = sink_ref[g * GQA_GROUP + hh] * LOG2E
            sh = s[hh * BLOCK:(hh + 1) * BLOCK]
            sh = jnp.concatenate([jnp.where(mask_prev, sh[:, :BLOCK], -jnp.inf), sh[:, BLOCK:2 * BLOCK],
                                  jnp.where(mask_next, sh[:, 2 * BLOCK:], -jnp.inf)], axis=1)
            m = jnp.maximum(jnp.max(sh, axis=-1, keepdims=True), sk)
            p = jnp.exp2(sh - m)
            denom = jnp.sum(p, axis=-1, keepdims=True) + jnp.exp2(sk - m)
            ps.append(p.astype(BF16))
            rs.append(1.0 / denom)
        p = jnp.concatenate(ps, axis=0)
        o = jnp.dot(p, vw, preferred_element_type=F32)
        return [o[hh * BLOCK:(hh + 1) * BLOCK] * rs[hh] for hh in range(GQA_GROUP)]

    pending = [scores(*u) for u in units[:ATTN_LOOKAHEAD]]
    outs = []
    for n, (j, g) in enumerate(units):
        if n + ATTN_LOOKAHEAD < len(units):
            pending.append(scores(*units[n + ATTN_LOOKAHEAD]))
        outs.append(softmax_pv(j, g, pending.pop(0)))
        if g == N_KV_HEADS - 1:
            a = jnp.concatenate([jnp.where(low, outs[0][hh], outs[1][hh]) for hh in range(GQA_GROUP)], axis=1)
            o_ref[0, j * BLOCK:(j + 1) * BLOCK, :] = _rms(a, gain, ATTN_WIDTH).astype(BF16)
            outs = []


def _attention(sink, q, k, v, out_gain):
    nb, s, _ = q.shape
    t = ATTN_TILE
    r = t // BLOCK
    last = s // BLOCK - 1
    main = lambda w: pl.BlockSpec((1, t, w), lambda b, i, *_: (b, i, 0))
    prev = pl.BlockSpec((1, BLOCK, KV_WIDTH), lambda b, i, *_: (b, jnp.maximum(i * r - 1, 0), 0))
    nxt = pl.BlockSpec((1, BLOCK, KV_WIDTH), lambda b, i, *_: (b, jnp.minimum((i + 1) * r, last), 0))
    grid_spec = pltpu.PrefetchScalarGridSpec(
        num_scalar_prefetch=1,
        grid=(nb, s // t),
        in_specs=[main(ATTN_WIDTH), prev, main(KV_WIDTH), nxt, prev, main(KV_WIDTH), nxt,
                  pl.BlockSpec((1, ATTN_WIDTH), lambda b, i, *_: (0, 0))],
        out_specs=main(ATTN_WIDTH),
    )
    return pl.pallas_call(
        functools.partial(_attn_kernel, seq=s),
        grid_spec=grid_spec,
        out_shape=jax.ShapeDtypeStruct((nb, s, ATTN_WIDTH), BF16),
        compiler_params=_COMPILER_PARAMS,
        name="attention",
    )(sink, q, k, k, k, v, v, v, out_gain)


def _dft1_kernel(m1_ref, cs_ref, u_ref, y_ref):
    m1 = m1_ref[...]
    cs = cs_ref[...]

    def channel_dft(j):
        slabs = []
        for t in range(F_SLABS):
            rows = u_ref.at[0, t].reshape(DFT_N * S1_GROUP, LANES)
            slabs.append(rows[pl.ds(j, DFT_N, stride=S1_GROUP), :].astype(BF16))
        per = MXU_DIM // LANES
        halves = [jnp.dot(jnp.concatenate(slabs[c * per:(c + 1) * per], axis=1), cs, preferred_element_type=F32)
                  for c in range(F_SLABS // per)]
        a = jnp.concatenate([h[:, :MXU_DIM] for h in halves], axis=1)
        b = jnp.concatenate([h[:, MXU_DIM:] for h in halves], axis=1)
        return jnp.concatenate([a, b], axis=0).astype(BF16)

    ab = [channel_dft(j) for j in range(S1_GROUP)]
    for j in range(S1_GROUP):
        y = jnp.dot(m1, ab[j], preferred_element_type=F32)
        y_ref[0, :, j * FOURIER_WIDTH:(j + 1) * FOURIER_WIDTH] = y.astype(BF16)


def _dft_stage1(m1, cs, u):
    nb = u.shape[0]
    groups = DFT_N // S1_GROUP
    u = u.reshape(nb, F_SLABS, DFT_N, groups, S1_GROUP, LANES)
    cols = S1_GROUP * FOURIER_WIDTH
    return pl.pallas_call(
        _dft1_kernel,
        grid=(nb, groups),
        in_specs=[_const_spec((2 * DFT_N, 2 * DFT_N)), _const_spec((MXU_DIM, 2 * MXU_DIM)),
                  pl.BlockSpec((1, F_SLABS, DFT_N, 1, S1_GROUP, LANES), lambda n, g: (n, 0, 0, g, 0, 0))],
        out_specs=pl.BlockSpec((1, 2 * DFT_N, cols), lambda n, g: (n, 0, g)),
        out_shape=jax.ShapeDtypeStruct((nb, 2 * DFT_N, DFT_N * FOURIER_WIDTH), BF16),
        compiler_params=_COMPILER_PARAMS,
        name="dft_stage1",
    )(m1, cs, u)


def _dft2_kernel(cn_ref, sn_ref, wc_ref, ws_ref, y_ref, wf_ref, g_ref, o_ref):
    cn = cn_ref[...]
    sn = sn_ref[...]
    wf = wf_ref[...]
    gain = g_ref[...]

    def seq_dft(j):
        wc = wc_ref[j:j + 1, :]
        ws = ws_ref[j:j + 1, :]
        gc = cn * wc - sn * ws
        gs = sn * wc + cn * ws
        g = jnp.concatenate([gc, gs], axis=1).astype(BF16)
        y = jnp.concatenate([y_ref[0, 0, j], y_ref[0, 1, j]], axis=0)
        return jnp.dot(g, y, preferred_element_type=F32).astype(BF16)

    xr = [seq_dft(j) for j in range(S2_K1)]
    fs = [jnp.dot(xr[j], wf, preferred_element_type=F32) for j in range(S2_K1)]
    for j in range(S2_K1):
        fn = _rms(fs[j], gain, FOURIER_WIDTH)
        for t in range(F_SLABS):
            rows = o_ref.at[0, t].reshape(DFT_N * S2_K1, LANES)
            rows[pl.ds(j, DFT_N, stride=S2_K1), :] = fn[:, t * LANES:(t + 1) * LANES]


def _dft_stage2(cn, sn, wc, ws, y, wf_bd, out_gain):
    nb = y.shape[0]
    y = y.reshape(nb, 2, DFT_N, DFT_N, FOURIER_WIDTH)
    groups = DFT_N // S2_K1
    tw = pl.BlockSpec((S2_K1, DFT_N), lambda n, k: (k, 0))
    out = pl.pallas_call(
        _dft2_kernel,
        grid=(nb, DFT_N // S2_K1),
        in_specs=[_const_spec((DFT_N, DFT_N)), _const_spec((DFT_N, DFT_N)), tw, tw,
                  pl.BlockSpec((1, 2, S2_K1, DFT_N, FOURIER_WIDTH), lambda n, k: (n, 0, k, 0, 0)),
                  _const_spec((FOURIER_WIDTH, FOURIER_WIDTH)), _const_spec((1, FOURIER_WIDTH))],
        out_specs=pl.BlockSpec((1, F_SLABS, DFT_N, 1, S2_K1, LANES), lambda n, k: (n, 0, 0, k, 0, 0)),
        out_shape=jax.ShapeDtypeStruct((nb, F_SLABS, DFT_N, groups, S2_K1, LANES), F32),
        compiler_params=_COMPILER_PARAMS,
        name="dft_stage2",
    )(cn, sn, wc, ws, y, wf_bd, out_gain)
    return out.reshape(nb, F_SLABS, DFT_N * DFT_N, LANES)


def _ffn_kernel(xm_ref, xp_ref, xn_ref, am_ref, ap_ref, an_ref, fm_ref, fp_ref, fn_ref, p_ref,
                wout_ref, gffn_ref, wup_ref, cw_ref, cb_ref, wdown_ref, wple_ref, gple_ref, wgate_ref,
                bgate_ref, o_ref):
    i = pl.program_id(1)
    nt = pl.num_programs(1)
    t = FFN_TILE
    ext = t + 2 * FFN_HALO
    trim = FFN_HALO - UP_HALO
    up_ext = t + 2 * UP_HALO
    x_all = jnp.concatenate([xp_ref[0], xm_ref[0], xn_ref[0]], axis=0)
    a_all = jnp.concatenate([ap_ref[0], am_ref[0], an_ref[0]], axis=0)
    f_all = [jnp.concatenate([fp_ref[0, s], fm_ref[0, s], fn_ref[0, s]], axis=0).astype(BF16)
             for s in range(F_SLABS)]
    mixed_all = jnp.concatenate([a_all] + f_all, axis=1)
    row = lax.broadcasted_iota(jnp.int32, (ext, 1), 0)

    def prologue(s):
        r0 = s * t
        x1 = x_all[r0:r0 + ext] + jnp.dot(mixed_all[r0:r0 + ext], wout_ref[...], preferred_element_type=F32)
        e = jnp.dot(p_ref[0, r0:r0 + t].astype(BF16), wple_ref[...], preferred_element_type=F32)
        e = _rms(e, gple_ref[...], D_MODEL)
        h = _rms(x1, gffn_ref[...], D_MODEL)
        first_valid = jnp.where(i == 0, FFN_HALO, 0) if s == 0 else 0
        end_valid = jnp.where(i == nt - 1, t + FFN_HALO, ext) if s == FFN_SUBTILES - 1 else ext
        h = jnp.where((row >= first_valid) & (row < end_valid), h, 0.0)
        return x1[FFN_HALO:FFN_HALO + t], e, h[trim:trim + up_ext].astype(BF16)

    def up_proj(h, j):
        return tuple(jnp.dot(h, wup_ref[:, c0:c0 + FFN_CHUNK], preferred_element_type=F32)
                     for c0 in (j * FFN_CHUNK, D_FF + j * FFN_CHUNK))

    def conv(hu, c0):
        cw = cw_ref[:, c0:c0 + FFN_CHUNK]
        before = pltpu.roll(hu, 1, 0)[UP_HALO:UP_HALO + t]
        after = pltpu.roll(hu, up_ext - 1, 0)[UP_HALO:UP_HALO + t]
        return (before * cw[0:1] + hu[UP_HALO:UP_HALO + t] * cw[1:2] + after * cw[2:3]
                + cb_ref[:, c0:c0 + FFN_CHUNK])

    def epilogue(s, x1, e, acc):
        x2 = x1 + acc
        gate = jax.nn.sigmoid(jnp.dot(x2.astype(BF16), wgate_ref[...], preferred_element_type=F32)
                              + bgate_ref[...])
        o_ref[0, s * t:(s + 1) * t, :] = x2 + gate * e

    units = [(s, j) for s in range(FFN_SUBTILES) for j in range(N_FFN_CHUNKS)]
    state = {}
    pending = []

    def issue(n):
        s, j = units[n]
        if j == 0:
            state[s] = list(prologue(s)) + [jnp.zeros((t, D_MODEL), F32)]
        pending.append(up_proj(state[s][2], j))

    for n in range(FFN_LOOKAHEAD):
        issue(n)
    for n, (s, j) in enumerate(units):
        if n + FFN_LOOKAHEAD < len(units):
            issue(n + FFN_LOOKAHEAD)
        hu_gate, hu_up = pending.pop(0)
        gate = conv(hu_gate, j * FFN_CHUNK)
        up = conv(hu_up, D_FF + j * FFN_CHUNK)
        act = (jax.nn.silu(gate) * up).astype(BF16)
        state[s][3] = state[s][3] + jnp.dot(act, wdown_ref[j * FFN_CHUNK:(j + 1) * FFN_CHUNK, :],
                                            preferred_element_type=F32)
        if j == N_FFN_CHUNKS - 1:
            x1, e, _, acc = state.pop(s)
            epilogue(s, x1, e, acc)


def _ffn(x, a, f, p, w_out, ffn_gain, w_up, conv_w, conv_b, w_down, w_ple, ple_gain, w_gate, b_gate):
    nb, s, _ = x.shape
    t = FFN_SUBTILES * FFN_TILE
    r = t // FFN_HALO
    last = s // FFN_HALO - 1

    def specs(w):
        return [pl.BlockSpec((1, t, w), lambda b, i: (b, i, 0)),
                pl.BlockSpec((1, FFN_HALO, w), lambda b, i: (b, jnp.maximum(i * r - 1, 0), 0)),
                pl.BlockSpec((1, FFN_HALO, w), lambda b, i: (b, jnp.minimum((i + 1) * r, last), 0))]

    f_specs = [pl.BlockSpec((1, F_SLABS, t, LANES), lambda b, i: (b, 0, i, 0)),
               pl.BlockSpec((1, F_SLABS, FFN_HALO, LANES), lambda b, i: (b, 0, jnp.maximum(i * r - 1, 0), 0)),
               pl.BlockSpec((1, F_SLABS, FFN_HALO, LANES), lambda b, i: (b, 0, jnp.minimum((i + 1) * r, last), 0))]

    in_specs = (specs(D_MODEL) + specs(ATTN_WIDTH) + f_specs
                + [pl.BlockSpec((1, t, PLE_DIM), lambda b, i: (b, i, 0)),
                   _const_spec((D_MODEL, D_MODEL)), _const_spec((1, D_MODEL)),
                   _const_spec((D_MODEL, 2 * D_FF)), _const_spec((3, 2 * D_FF)), _const_spec((1, 2 * D_FF)),
                   _const_spec((D_FF, D_MODEL)), _const_spec((PLE_DIM, D_MODEL)), _const_spec((1, D_MODEL)),
                   _const_spec((D_MODEL, D_MODEL)), _const_spec((1, D_MODEL))])
    return pl.pallas_call(
        _ffn_kernel,
        grid=(nb, s // t),
        in_specs=in_specs,
        out_specs=pl.BlockSpec((1, t, D_MODEL), lambda b, i: (b, i, 0)),
        out_shape=jax.ShapeDtypeStruct((nb, s, D_MODEL), F32),
        compiler_params=_COMPILER_PARAMS,
        name="ffn",
    )(x, x, x, a, a, a, f, f, f, p, w_out, ffn_gain, w_up, conv_w, conv_b, w_down, w_ple, ple_gain,
      w_gate, b_gate)


def _dft_tables():
    n = np.arange(DFT_N)
    ang = 2.0 * np.pi * ((n[:, None] * n[None, :]) % DFT_N) / DFT_N
    cn, sn = np.cos(ang), np.sin(ang)
    m1 = np.block([[cn, -sn], [-sn, -cn]])
    seq = DFT_N * DFT_N
    tw = 2.0 * np.pi * ((n[:, None] * n[None, :]) % seq) / seq
    ortho = 1.0 / np.sqrt(seq * FOURIER_GROUP_DIM)
    c = np.arange(FOURIER_GROUP_DIM)
    ang_c = 2.0 * np.pi * ((c[:, None] * c[None, :]) % FOURIER_GROUP_DIM) / FOURIER_GROUP_DIM
    eye = np.eye(MXU_DIM // FOURIER_GROUP_DIM)
    cs = np.concatenate([np.kron(eye, np.cos(ang_c)), np.kron(eye, np.sin(ang_c))], axis=1)
    hsum = np.kron(np.eye(MXU_DIM // HEAD_DIM), np.ones((HEAD_DIM, HEAD_DIM)))
    f32 = lambda a: jnp.asarray(a, dtype=F32)
    return (f32(m1), f32(cn * ortho), f32(sn * ortho), f32(np.cos(tw)), f32(np.sin(tw)), f32(cs), f32(hsum))


def _rope_tables(seq):
    inv_freq = ROPE_THETA ** (-jnp.arange(0, ROT_DIM, 2, dtype=F32) / ROT_DIM)
    inv_head = jnp.concatenate([inv_freq, inv_freq, jnp.zeros((HEAD_DIM - ROT_DIM,), F32)])
    inv_lane = jnp.concatenate([inv_head] * (LANES // HEAD_DIM))
    ang = jnp.arange(seq, dtype=F32)[:, None] * inv_lane[None, :]
    return jnp.cos(ang), jnp.sin(ang)


_HEAD_ORDER = tuple(g * GQA_GROUP + i for i in range(GQA_GROUP) for g in range(N_KV_HEADS))


def _permute_heads(a, axis):
    blocks = [lax.slice_in_dim(a, h * HEAD_DIM, (h + 1) * HEAD_DIM, axis=axis) for h in _HEAD_ORDER]
    return jnp.concatenate(blocks, axis=axis)


def _block_diag(w):
    g, c, e = w.shape
    eye = jnp.eye(g, dtype=w.dtype)
    return (eye[:, None, :, None] * w[:, :, None, :]).reshape(g * c, g * e)


def _layer(x, p, prm, tables):
    (attn_norm, w_in, q_norm, k_norm, sink, w_fourier, attn_out_norm, fourier_out_norm, w_out, ffn_norm,
     w_up, conv_w, conv_b, w_down, w_ple, ple_norm, w_ple_gate, b_ple_gate) = prm
    m1, cn, sn, wc, ws, cs_bd, hsum = tables
    seq = x.shape[1]
    assert seq == DFT_N * DFT_N
    cos_t, sin_t = _rope_tables(seq)
    row = lambda v: v.reshape(1, -1)
    two = lambda v: jnp.concatenate([v] * (LANES // HEAD_DIM)).reshape(1, LANES)

    w_in_p = jnp.concatenate([_permute_heads(w_in[:, :ATTN_WIDTH], 1), w_in[:, ATTN_WIDTH:]], axis=1)
    w_out_p = jnp.concatenate([_permute_heads(w_out[:ATTN_WIDTH], 0), w_out[ATTN_WIDTH:]], axis=0)
    q, k, v, u = _inproj(x, row(attn_norm), w_in_p.astype(BF16), two(q_norm), two(k_norm), cos_t, sin_t,
                         hsum.astype(BF16))
    attn = _attention(sink, q, k, v, row(_permute_heads(attn_out_norm, 0)))
    y = _dft_stage1(m1.astype(BF16), cs_bd.astype(BF16), u)
    four = _dft_stage2(cn, sn, wc, ws, y, _block_diag(w_fourier).astype(BF16), row(fourier_out_norm))
    return _ffn(x, attn, four, p, w_out_p.astype(BF16), row(ffn_norm), w_up.astype(BF16), conv_w, row(conv_b),
                w_down.astype(BF16), w_ple.astype(BF16), row(ple_norm), w_ple_gate.astype(BF16),
                row(b_ple_gate))


def kernel(x_prompt, x_sample, p_prompt, p_sample, attn_norm, w_in, q_norm, k_norm, sink, w_fourier,
           attn_out_norm, fourier_out_norm, w_out, ffn_norm, w_up, conv_w, conv_b, w_down, w_ple, ple_norm,
           w_ple_gate, b_ple_gate):
    stacked = (attn_norm, w_in, q_norm, k_norm, sink, w_fourier, attn_out_norm, fourier_out_norm, w_out,
               ffn_norm, w_up, conv_w, conv_b, w_down, w_ple, ple_norm, w_ple_gate, b_ple_gate)
    tables = _dft_tables()
    y_prompt, y_sample = x_prompt, x_sample
    for i in range(attn_norm.shape[0]):
        prm = tuple(w[i] for w in stacked)
        y_prompt = _layer(y_prompt, p_prompt[i], prm, tables)
        y_sample = _layer(y_sample, p_sample[i], prm, tables)
    return (y_prompt, y_sample)
```

```python
import functools

import numpy as np
import jax
import jax.numpy as jnp
from jax import lax
from jax.experimental import pallas as pl
from jax.experimental.pallas import tpu as pltpu

F32 = jnp.float32
BF16 = jnp.bfloat16

D_MODEL = 1024
HEAD_DIM = 64
N_HEADS = 8
N_KV_HEADS = 2
GQA_GROUP = N_HEADS // N_KV_HEADS
ATTN_WIDTH = N_HEADS * HEAD_DIM
KV_WIDTH = N_KV_HEADS * HEAD_DIM
FOURIER_WIDTH = 512
FOURIER_GROUP_DIM = 64
N_FOURIER_GROUPS = FOURIER_WIDTH // FOURIER_GROUP_DIM
IN_WIDTH = ATTN_WIDTH + 2 * KV_WIDTH + FOURIER_WIDTH
BLOCK = 128
ROPE_THETA = 500000.0
ROT_DIM = HEAD_DIM // 4
ROT_HALF = ROT_DIM // 2
D_FF = 2816
PLE_DIM = 256
EPS = 1e-6
LOG2E = 1.4426950408889634

LANES = 128
MXU_DIM = 256
BF16_ROWS = 16
VMEM_LIMIT_BYTES = 56 * 1024 * 1024

DFT_N = 128

IN_TILE = 1024
IN_SUB = 128
ATTN_TILE = 1024
ATTN_LOOKAHEAD = 2
FFN_LOOKAHEAD = 2
F32_ROWS = 8
S1_GROUP = F32_ROWS
F_SLABS = FOURIER_WIDTH // LANES
S2_K1 = F32_ROWS
FFN_TILE = 256
FFN_SUBTILES = 2
FFN_HALO = BF16_ROWS
UP_HALO = F32_ROWS
FFN_CHUNK = 256
N_FFN_CHUNKS = D_FF // FFN_CHUNK


_COMPILER_PARAMS = pltpu.CompilerParams(
    dimension_semantics=("parallel", "parallel"),
    vmem_limit_bytes=VMEM_LIMIT_BYTES,
)


def _rms(x, gain, n):
    ms = jnp.sum(x * x, axis=-1, keepdims=True) * (1.0 / n)
    return x * lax.rsqrt(ms + EPS) * gain


def _const_spec(shape):
    zeros = (0,) * len(shape)
    return pl.BlockSpec(shape, lambda *_: zeros, pipeline_mode=pl.Buffered(1))


def _inproj_kernel(x_ref, g_ref, win_ref, qg_ref, kg_ref, cos_ref, sin_ref, hsum_ref, q_ref, k_ref, v_ref, u_ref):
    d = lax.broadcasted_iota(jnp.int32, (IN_SUB, LANES), 1) % HEAD_DIM
    scale = HEAD_DIM ** -0.5 * LOG2E

    def project(r0):
        h = _rms(x_ref[0, r0:r0 + IN_SUB], g_ref[...], D_MODEL).astype(BF16)
        return jnp.dot(h, win_ref[...], preferred_element_type=F32)

    def finish(r0, z):
        cos_t = cos_ref[r0:r0 + IN_SUB]
        sin_t = sin_ref[r0:r0 + IN_SUB]
        sin_lo = jnp.where(d < ROT_HALF, -sin_t, 0.0)
        sin_hi = jnp.where(d >= ROT_HALF, sin_t, 0.0)

        def norm_rope(t, gain):
            w = t.shape[1]
            wide = lambda a: jnp.concatenate([a] * (w // LANES), axis=1)
            ssq = jnp.dot((t * t).astype(BF16), hsum_ref[:w, :w], preferred_element_type=F32)
            tn = t * lax.rsqrt(ssq * (1.0 / HEAD_DIM) + EPS) * wide(gain)
            up = pltpu.roll(tn, w - ROT_HALF, 1)
            dn = pltpu.roll(tn, ROT_HALF, 1)
            return tn * wide(cos_t) + up * wide(sin_lo) + dn * wide(sin_hi)

        for t in range(ATTN_WIDTH // MXU_DIM):
            qt = norm_rope(z[:, t * MXU_DIM:(t + 1) * MXU_DIM], qg_ref[...])
            q_ref[0, r0:r0 + IN_SUB, t * MXU_DIM:(t + 1) * MXU_DIM] = (qt * scale).astype(BF16)
        k_ref[0, r0:r0 + IN_SUB] = norm_rope(z[:, ATTN_WIDTH:ATTN_WIDTH + KV_WIDTH], kg_ref[...]).astype(BF16)
        v_ref[0, r0:r0 + IN_SUB] = z[:, ATTN_WIDTH + KV_WIDTH:ATTN_WIDTH + 2 * KV_WIDTH].astype(BF16)
        for t in range(F_SLABS):
            lo = ATTN_WIDTH + 2 * KV_WIDTH + t * LANES
            u_ref[0, t, r0:r0 + IN_SUB] = z[:, lo:lo + LANES]

    starts = list(range(0, IN_TILE, IN_SUB))
    z = project(starts[0])
    for n, r0 in enumerate(starts):
        z_next = project(starts[n + 1]) if n + 1 < len(starts) else None
        finish(r0, z)
        z = z_next


def _inproj(x, attn_norm, w_in, q_gain, k_gain, cos_t, sin_t, hsum):
    nb, s, _ = x.shape
    t = IN_TILE
    tok = lambda w: pl.BlockSpec((1, t, w), lambda b, i: (b, i, 0))
    pos = pl.BlockSpec((t, LANES), lambda b, i: (i, 0))
    out_shape = (
        jax.ShapeDtypeStruct((nb, s, ATTN_WIDTH), BF16),
        jax.ShapeDtypeStruct((nb, s, KV_WIDTH), BF16),
        jax.ShapeDtypeStruct((nb, s, KV_WIDTH), BF16),
        jax.ShapeDtypeStruct((nb, F_SLABS, s, LANES), F32),
    )
    return pl.pallas_call(
        _inproj_kernel,
        grid=(nb, s // t),
        in_specs=[tok(D_MODEL), _const_spec((1, D_MODEL)), _const_spec((D_MODEL, IN_WIDTH)),
                  _const_spec((1, LANES)), _const_spec((1, LANES)), pos, pos, _const_spec((MXU_DIM, MXU_DIM))],
        out_specs=(tok(ATTN_WIDTH), tok(KV_WIDTH), tok(KV_WIDTH),
                   pl.BlockSpec((1, F_SLABS, t, LANES), lambda b, i: (b, 0, i, 0))),
        out_shape=out_shape,
        compiler_params=_COMPILER_PARAMS,
        name="inproj",
    )(x, attn_norm, w_in, q_gain, k_gain, cos_t, sin_t, hsum)


def _attn_kernel(sink_ref, q_ref, kp_ref, km_ref, kn_ref, vp_ref, vm_ref, vn_ref, g_ref, o_ref, *, seq):
    i = pl.program_id(1)
    kcat = jnp.concatenate([kp_ref[0], km_ref[0], kn_ref[0]], axis=0)
    vcat = jnp.concatenate([vp_ref[0], vm_ref[0], vn_ref[0]], axis=0)
    nkeys = 3 * BLOCK
    diff = (lax.broadcasted_iota(jnp.int32, (BLOCK, BLOCK), 1)
            - lax.broadcasted_iota(jnp.int32, (BLOCK, BLOCK), 0))
    lane = lax.broadcasted_iota(jnp.int32, (BLOCK, LANES), 1)
    low = lane < HEAD_DIM
    gain = g_ref[...]
    units = [(j, g) for j in range(ATTN_TILE // BLOCK) for g in range(N_KV_HEADS)]

    def scores(j, g):
        kw = kcat[j * BLOCK:j * BLOCK + nkeys]
        sel = low if g == 0 else jnp.logical_not(low)
        qs = []
        for hh in range(GQA_GROUP):
            qt = q_ref[0, j * BLOCK:(j + 1) * BLOCK, hh * LANES:(hh + 1) * LANES]
            qs.append(jnp.where(sel, qt, jnp.zeros_like(qt)))
        qg = jnp.concatenate(qs, axis=0)
        return lax.dot_general(qg, kw, (((1,), (1,)), ((), ())), preferred_element_type=F32)

    def softmax_pv(j, g, s):
        qblk = i * (ATTN_TILE // BLOCK) + j
        lo = jnp.where(qblk > 0, 0, BLOCK)
        hi = jnp.where(qblk < seq // BLOCK - 1, 0, -BLOCK)
        mask_prev = diff >= lo
        mask_next = diff <= hi
        vw = vcat[j * BLOCK:j * BLOCK + nkeys]
        ps, rs = [], []
        for hh in range(GQA_GROUP):
            sk = sink_ref[g * GQA_GROUP + hh] * LOG2E
            sh = s[hh * BLOCK:(hh + 1) * BLOCK]
            sh = jnp.concatenate([jnp.where(mask_prev, sh[:, :BLOCK], -jnp.inf), sh[:, BLOCK:2 * BLOCK],
                                  jnp.where(mask_next, sh[:, 2 * BLOCK:], -jnp.inf)], axis=1)
            m = jnp.maximum(jnp.max(sh, axis=-1, keepdims=True), sk)
            p = jnp.exp2(sh - m)
            denom = jnp.sum(p, axis=-1, keepdims=True) + jnp.exp2(sk - m)
            ps.append(p.astype(BF16))
            rs.append(1.0 / denom)
        p = jnp.concatenate(ps, axis=0)
        o = jnp.dot(p, vw, preferred_element_type=F32)
        return [o[hh * BLOCK:(hh + 1) * BLOCK] * rs[hh] for hh in range(GQA_GROUP)]

    pending = [scores(*u) for u in units[:ATTN_LOOKAHEAD]]
    outs = []
    for n, (j, g) in enumerate(units):
        if n + ATTN_LOOKAHEAD < len(units):
            pending.append(scores(*units[n + ATTN_LOOKAHEAD]))
        outs.append(softmax_pv(j, g, pending.pop(0)))
        if g == N_KV_HEADS - 1:
            a = jnp.concatenate([jnp.where(low, outs[0][hh], outs[1][hh]) for hh in range(GQA_GROUP)], axis=1)
            o_ref[0, j * BLOCK:(j + 1) * BLOCK, :] = _rms(a, gain, ATTN_WIDTH).astype(BF16)
            outs = []


def _attention(sink, q, k, v, out_gain):
    nb, s, _ = q.shape
    t = ATTN_TILE
    r = t // BLOCK
    last = s // BLOCK - 1
    main = lambda w: pl.BlockSpec((1, t, w), lambda b, i, *_: (b, i, 0))
    prev = pl.BlockSpec((1, BLOCK, KV_WIDTH), lambda b, i, *_: (b, jnp.maximum(i * r - 1, 0), 0))
    nxt = pl.BlockSpec((1, BLOCK, KV_WIDTH), lambda b, i, *_: (b, jnp.minimum((i + 1) * r, last), 0))
    grid_spec = pltpu.PrefetchScalarGridSpec(
        num_scalar_prefetch=1,
        grid=(nb, s // t),
        in_specs=[main(ATTN_WIDTH), prev, main(KV_WIDTH), nxt, prev, main(KV_WIDTH), nxt,
                  pl.BlockSpec((1, ATTN_WIDTH), lambda b, i, *_: (0, 0))],
        out_specs=main(ATTN_WIDTH),
    )
    return pl.pallas_call(
        functools.partial(_attn_kernel, seq=s),
        grid_spec=grid_spec,
        out_shape=jax.ShapeDtypeStruct((nb, s, ATTN_WIDTH), BF16),
        compiler_params=_COMPILER_PARAMS,
        name="attention",
    )(sink, q, k, k, k, v, v, v, out_gain)


def _dft1_kernel(m1_ref, cs_ref, u_ref, y_ref):
    m1 = m1_ref[...]
    cs = cs_ref[...]

    def channel_dft(j):
        slabs = []
        for t in range(F_SLABS):
            rows = u_ref.at[0, t].reshape(DFT_N * S1_GROUP, LANES)
            slabs.append(rows[pl.ds(j, DFT_N, stride=S1_GROUP), :].astype(BF16))
        per = MXU_DIM // LANES
        halves = [jnp.dot(jnp.concatenate(slabs[c * per:(c + 1) * per], axis=1), cs, preferred_element_type=F32)
                  for c in range(F_SLABS // per)]
        a = jnp.concatenate([h[:, :MXU_DIM] for h in halves], axis=1)
        b = jnp.concatenate([h[:, MXU_DIM:] for h in halves], axis=1)
        return jnp.concatenate([a, b], axis=0).astype(BF16)

    ab = [channel_dft(j) for j in range(S1_GROUP)]
    for j in range(S1_GROUP):
        y = jnp.dot(m1, ab[j], preferred_element_type=F32)
        y_ref[0, :, j * FOURIER_WIDTH:(j + 1) * FOURIER_WIDTH] = y.astype(BF16)


def _dft_stage1(m1, cs, u):
    nb = u.shape[0]
    groups = DFT_N // S1_GROUP
    u = u.reshape(nb, F_SLABS, DFT_N, groups, S1_GROUP, LANES)
    cols = S1_GROUP * FOURIER_WIDTH
    return pl.pallas_call(
        _dft1_kernel,
        grid=(nb, groups),
        in_specs=[_const_spec((2 * DFT_N, 2 * DFT_N)), _const_spec((MXU_DIM, 2 * MXU_DIM)),
                  pl.BlockSpec((1, F_SLABS, DFT_N, 1, S1_GROUP, LANES), lambda n, g: (n, 0, 0, g, 0, 0))],
        out_specs=pl.BlockSpec((1, 2 * DFT_N, cols), lambda n, g: (n, 0, g)),
        out_shape=jax.ShapeDtypeStruct((nb, 2 * DFT_N, DFT_N * FOURIER_WIDTH), BF16),
        compiler_params=_COMPILER_PARAMS,
        name="dft_stage1",
    )(m1, cs, u)


def _dft2_kernel(cn_ref, sn_ref, wc_ref, ws_ref, y_ref, wf_ref, g_ref, o_ref):
    cn = cn_ref[...]
    sn = sn_ref[...]
    wf = wf_ref[...]
    gain = g_ref[...]

    def seq_dft(j):
        wc = wc_ref[j:j + 1, :]
        ws = ws_ref[j:j + 1, :]
        gc = cn * wc - sn * ws
        gs = sn * wc + cn * ws
        g = jnp.concatenate([gc, gs], axis=1).astype(BF16)
        y = jnp.concatenate([y_ref[0, 0, j], y_ref[0, 1, j]], axis=0)
        return jnp.dot(g, y, preferred_element_type=F32).astype(BF16)

    xr = [seq_dft(j) for j in range(S2_K1)]
    fs = [jnp.dot(xr[j], wf, preferred_element_type=F32) for j in range(S2_K1)]
    for j in range(S2_K1):
        fn = _rms(fs[j], gain, FOURIER_WIDTH)
        for t in range(F_SLABS):
            rows = o_ref.at[0, t].reshape(DFT_N * S2_K1, LANES)
            rows[pl.ds(j, DFT_N, stride=S2_K1), :] = fn[:, t * LANES:(t + 1) * LANES]


def _dft_stage2(cn, sn, wc, ws, y, wf_bd, out_gain):
    nb = y.shape[0]
    y = y.reshape(nb, 2, DFT_N, DFT_N, FOURIER_WIDTH)
    groups = DFT_N // S2_K1
    tw = pl.BlockSpec((S2_K1, DFT_N), lambda n, k: (k, 0))
    out = pl.pallas_call(
        _dft2_kernel,
        grid=(nb, DFT_N // S2_K1),
        in_specs=[_const_spec((DFT_N, DFT_N)), _const_spec((DFT_N, DFT_N)), tw, tw,
                  pl.BlockSpec((1, 2, S2_K1, DFT_N, FOURIER_WIDTH), lambda n, k: (n, 0, k, 0, 0)),
                  _const_spec((FOURIER_WIDTH, FOURIER_WIDTH)), _const_spec((1, FOURIER_WIDTH))],
        out_specs=pl.BlockSpec((1, F_SLABS, DFT_N, 1, S2_K1, LANES), lambda n, k: (n, 0, 0, k, 0, 0)),
        out_shape=jax.ShapeDtypeStruct((nb, F_SLABS, DFT_N, groups, S2_K1, LANES), F32),
        compiler_params=_COMPILER_PARAMS,
        name="dft_stage2",
    )(cn, sn, wc, ws, y, wf_bd, out_gain)
    return out.reshape(nb, F_SLABS, DFT_N * DFT_N, LANES)


def _ffn_kernel(xm_ref, xp_ref, xn_ref, am_ref, ap_ref, an_ref, fm_ref, fp_ref, fn_ref, p_ref,
                wout_ref, gffn_ref, wup_ref, cw_ref, cb_ref, wdown_ref, wple_ref, gple_ref, wgate_ref,
                bgate_ref, o_ref):
    i = pl.program_id(1)
    nt = pl.num_programs(1)
    t = FFN_TILE
    ext = t + 2 * FFN_HALO
    trim = FFN_HALO - UP_HALO
    up_ext = t + 2 * UP_HALO
    x_all = jnp.concatenate([xp_ref[0], xm_ref[0], xn_ref[0]], axis=0)
    a_all = jnp.concatenate([ap_ref[0], am_ref[0], an_ref[0]], axis=0)
    f_all = [jnp.concatenate([fp_ref[0, s], fm_ref[0, s], fn_ref[0, s]], axis=0).astype(BF16)
             for s in range(F_SLABS)]
    mixed_all = jnp.concatenate([a_all] + f_all, axis=1)
    row = lax.broadcasted_iota(jnp.int32, (ext, 1), 0)

    def prologue(s):
        r0 = s * t
        x1 = x_all[r0:r0 + ext] + jnp.dot(mixed_all[r0:r0 + ext], wout_ref[...], preferred_element_type=F32)
        e = jnp.dot(p_ref[0, r0:r0 + t].astype(BF16), wple_ref[...], preferred_element_type=F32)
        e = _rms(e, gple_ref[...], D_MODEL)
        h = _rms(x1, gffn_ref[...], D_MODEL)
        first_valid = jnp.where(i == 0, FFN_HALO, 0) if s == 0 else 0
        end_valid = jnp.where(i == nt - 1, t + FFN_HALO, ext) if s == FFN_SUBTILES - 1 else ext
        h = jnp.where((row >= first_valid) & (row < end_valid), h, 0.0)
        return x1[FFN_HALO:FFN_HALO + t], e, h[trim:trim + up_ext].astype(BF16)

    def up_proj(h, j):
        return tuple(jnp.dot(h, wup_ref[:, c0:c0 + FFN_CHUNK], preferred_element_type=F32)
                     for c0 in (j * FFN_CHUNK, D_FF + j * FFN_CHUNK))

    def conv(hu, c0):
        cw = cw_ref[:, c0:c0 + FFN_CHUNK]
        before = pltpu.roll(hu, 1, 0)[UP_HALO:UP_HALO + t]
        after = pltpu.roll(hu, up_ext - 1, 0)[UP_HALO:UP_HALO + t]
        return (before * cw[0:1] + hu[UP_HALO:UP_HALO + t] * cw[1:2] + after * cw[2:3]
                + cb_ref[:, c0:c0 + FFN_CHUNK])

    def epilogue(s, x1, e, acc):
        x2 = x1 + acc
        gate = jax.nn.sigmoid(jnp.dot(x2.astype(BF16), wgate_ref[...], preferred_element_type=F32)
                              + bgate_ref[...])
        o_ref[0, s * t:(s + 1) * t, :] = x2 + gate * e

    units = [(s, j) for s in range(FFN_SUBTILES) for j in range(N_FFN_CHUNKS)]
    state = {}
    pending = []

    def issue(n):
        s, j = units[n]
        if j == 0:
            state[s] = list(prologue(s)) + [jnp.zeros((t, D_MODEL), F32)]
        pending.append(up_proj(state[s][2], j))

    for n in range(FFN_LOOKAHEAD):
        issue(n)
    for n, (s, j) in enumerate(units):
        if n + FFN_LOOKAHEAD < len(units):
            issue(n + FFN_LOOKAHEAD)
        hu_gate, hu_up = pending.pop(0)
        gate = conv(hu_gate, j * FFN_CHUNK)
        up = conv(hu_up, D_FF + j * FFN_CHUNK)
        act = (jax.nn.silu(gate) * up).astype(BF16)
        state[s][3] = state[s][3] + jnp.dot(act, wdown_ref[j * FFN_CHUNK:(j + 1) * FFN_CHUNK, :],
                                            preferred_element_type=F32)
        if j == N_FFN_CHUNKS - 1:
            x1, e, _, acc = state.pop(s)
            epilogue(s, x1, e, acc)


def _ffn(x, a, f, p, w_out, ffn_gain, w_up, conv_w, conv_b, w_down, w_ple, ple_gain, w_gate, b_gate):
    nb, s, _ = x.shape
    t = FFN_SUBTILES * FFN_TILE
    r = t // FFN_HALO
    last = s // FFN_HALO - 1

    def specs(w):
        return [pl.BlockSpec((1, t, w), lambda b, i: (b, i, 0)),
                pl.BlockSpec((1, FFN_HALO, w), lambda b, i: (b, jnp.maximum(i * r - 1, 0), 0)),
                pl.BlockSpec((1, FFN_HALO, w), lambda b, i: (b, jnp.minimum((i + 1) * r, last), 0))]

    f_specs = [pl.BlockSpec((1, F_SLABS, t, LANES), lambda b, i: (b, 0, i, 0)),
               pl.BlockSpec((1, F_SLABS, FFN_HALO, LANES), lambda b, i: (b, 0, jnp.maximum(i * r - 1, 0), 0)),
               pl.BlockSpec((1, F_SLABS, FFN_HALO, LANES), lambda b, i: (b, 0, jnp.minimum((i + 1) * r, last), 0))]

    in_specs = (specs(D_MODEL) + specs(ATTN_WIDTH) + f_specs
                + [pl.BlockSpec((1, t, PLE_DIM), lambda b, i: (b, i, 0)),
                   _const_spec((D_MODEL, D_MODEL)), _const_spec((1, D_MODEL)),
                   _const_spec((D_MODEL, 2 * D_FF)), _const_spec((3, 2 * D_FF)), _const_spec((1, 2 * D_FF)),
                   _const_spec((D_FF, D_MODEL)), _const_spec((PLE_DIM, D_MODEL)), _const_spec((1, D_MODEL)),
                   _const_spec((D_MODEL, D_MODEL)), _const_spec((1, D_MODEL))])
    return pl.pallas_call(
        _ffn_kernel,
        grid=(nb, s // t),
        in_specs=in_specs,
        out_specs=pl.BlockSpec((1, t, D_MODEL), lambda b, i: (b, i, 0)),
        out_shape=jax.ShapeDtypeStruct((nb, s, D_MODEL), F32),
        compiler_params=_COMPILER_PARAMS,
        name="ffn",
    )(x, x, x, a, a, a, f, f, f, p, w_out, ffn_gain, w_up, conv_w, conv_b, w_down, w_ple, ple_gain,
      w_gate, b_gate)


def _dft_tables():
    n = np.arange(DFT_N)
    ang = 2.0 * np.pi * ((n[:, None] * n[None, :]) % DFT_N) / DFT_N
    cn, sn = np.cos(ang), np.sin(ang)
    m1 = np.block([[cn, -sn], [-sn, -cn]])
    seq = DFT_N * DFT_N
    tw = 2.0 * np.pi * ((n[:, None] * n[None, :]) % seq) / seq
    ortho = 1.0 / np.sqrt(seq * FOURIER_GROUP_DIM)
    c = np.arange(FOURIER_GROUP_DIM)
    ang_c = 2.0 * np.pi * ((c[:, None] * c[None, :]) % FOURIER_GROUP_DIM) / FOURIER_GROUP_DIM
    eye = np.eye(MXU_DIM // FOURIER_GROUP_DIM)
    cs = np.concatenate([np.kron(eye, np.cos(ang_c)), np.kron(eye, np.sin(ang_c))], axis=1)
    hsum = np.kron(np.eye(MXU_DIM // HEAD_DIM), np.ones((HEAD_DIM, HEAD_DIM)))
    f32 = lambda a: jnp.asarray(a, dtype=F32)
    return (f32(m1), f32(cn * ortho), f32(sn * ortho), f32(np.cos(tw)), f32(np.sin(tw)), f32(cs), f32(hsum))


def _rope_tables(seq):
    inv_freq = ROPE_THETA ** (-jnp.arange(0, ROT_DIM, 2, dtype=F32) / ROT_DIM)
    inv_head = jnp.concatenate([inv_freq, inv_freq, jnp.zeros((HEAD_DIM - ROT_DIM,), F32)])
    inv_lane = jnp.concatenate([inv_head] * (LANES // HEAD_DIM))
    ang = jnp.arange(seq, dtype=F32)[:, None] * inv_lane[None, :]
    return jnp.cos(ang), jnp.sin(ang)


_HEAD_ORDER = tuple(g * GQA_GROUP + i for i in range(GQA_GROUP) for g in range(N_KV_HEADS))


def _permute_heads(a, axis):
    blocks = [lax.slice_in_dim(a, h * HEAD_DIM, (h + 1) * HEAD_DIM, axis=axis) for h in _HEAD_ORDER]
    return jnp.concatenate(blocks, axis=axis)


def _block_diag(w):
    g, c, e = w.shape
    eye = jnp.eye(g, dtype=w.dtype)
    return (eye[:, None, :, None] * w[:, :, None, :]).reshape(g * c, g * e)


def _layer(xs, ps, prm, tables):
    (attn_norm, w_in, q_norm, k_norm, sink, w_fourier, attn_out_norm, fourier_out_norm, w_out, ffn_norm,
     w_up, conv_w, conv_b, w_down, w_ple, ple_norm, w_ple_gate, b_ple_gate) = prm
    m1, cn, sn, wc, ws, cs_bd, hsum = tables
    row = lambda v: v.reshape(1, -1)
    two = lambda v: jnp.concatenate([v] * (LANES // HEAD_DIM)).reshape(1, LANES)

    w_in_p = jnp.concatenate([_permute_heads(w_in[:, :ATTN_WIDTH], 1), w_in[:, ATTN_WIDTH:]], axis=1)
    w_out_p = jnp.concatenate([_permute_heads(w_out[:ATTN_WIDTH], 0), w_out[ATTN_WIDTH:]], axis=0)
    proj = []
    for x in xs:
        assert x.shape[1] == DFT_N * DFT_N
        cos_t, sin_t = _rope_tables(x.shape[1])
        proj.append(_inproj(x, row(attn_norm), w_in_p.astype(BF16), two(q_norm), two(k_norm), cos_t, sin_t,
                            hsum.astype(BF16)))
    ys = [_dft_stage1(m1.astype(BF16), cs_bd.astype(BF16), u) for _, _, _, u in proj]
    attns = [_attention(sink, q, k, v, row(_permute_heads(attn_out_norm, 0))) for q, k, v, _ in proj]
    fours = [_dft_stage2(cn, sn, wc, ws, y, _block_diag(w_fourier).astype(BF16), row(fourier_out_norm))
             for y in ys]
    return [_ffn(x, attn, four, p, w_out_p.astype(BF16), row(ffn_norm), w_up.astype(BF16), conv_w,
                 row(conv_b), w_down.astype(BF16), w_ple.astype(BF16), row(ple_norm),
                 w_ple_gate.astype(BF16), row(b_ple_gate))
            for x, attn, four, p in zip(xs, attns, fours, ps)]


def kernel(x_prompt, x_sample, p_prompt, p_sample, attn_norm, w_in, q_norm, k_norm, sink, w_fourier,
           attn_out_norm, fourier_out_norm, w_out, ffn_norm, w_up, conv_w, conv_b, w_down, w_ple, ple_norm,
           w_ple_gate, b_ple_gate):
    stacked = (attn_norm, w_in, q_norm, k_norm, sink, w_fourier, attn_out_norm, fourier_out_norm, w_out,
               ffn_norm, w_up, conv_w, conv_b, w_down, w_ple, ple_norm, w_ple_gate, b_ple_gate)
    tables = _dft_tables()
    y_prompt, y_sample = x_prompt, x_sample
    for i in range(attn_norm.shape[0]):
        prm = tuple(w[i] for w in stacked)
        y_prompt, y_sample = _layer([y_prompt, y_sample], [p_prompt[i], p_sample[i]], prm, tables)
    return (y_prompt, y_sample)
```

```python
import functools

import numpy as np
import jax
import jax.numpy as jnp
from jax import lax
from jax.experimental import pallas as pl
from jax.experimental.pallas import tpu as pltpu

F32 = jnp.float32
BF16 = jnp.bfloat16

D_MODEL = 1024
HEAD_DIM = 64
N_HEADS = 8
N_KV_HEADS = 2
GQA_GROUP = N_HEADS // N_KV_HEADS
ATTN_WIDTH = N_HEADS * HEAD_DIM
KV_WIDTH = N_KV_HEADS * HEAD_DIM
FOURIER_WIDTH = 512
FOURIER_GROUP_DIM = 64
N_FOURIER_GROUPS = FOURIER_WIDTH // FOURIER_GROUP_DIM
IN_WIDTH = ATTN_WIDTH + 2 * KV_WIDTH + FOURIER_WIDTH
BLOCK = 128
ROPE_THETA = 500000.0
ROT_DIM = HEAD_DIM // 4
ROT_HALF = ROT_DIM // 2
D_FF = 2816
PLE_DIM = 256
EPS = 1e-6
LOG2E = 1.4426950408889634

LANES = 128
MXU_DIM = 256
BF16_ROWS = 16
VMEM_LIMIT_BYTES = 56 * 1024 * 1024

DFT_N = 128

IN_TILE = 1024
IN_SUB = 128
ATTN_TILE = 1024
ATTN_LOOKAHEAD = 2
FFN_LOOKAHEAD = 2
F32_ROWS = 8
S1_GROUP = F32_ROWS
F_SLABS = FOURIER_WIDTH // LANES
S2_K1 = F32_ROWS
FFN_TILE = 256
FFN_SUBTILES = 2
FFN_HALO = BF16_ROWS
UP_HALO = F32_ROWS
FFN_CHUNK = 256
N_FFN_CHUNKS = D_FF // FFN_CHUNK
HU_SLOTS = FFN_LOOKAHEAD + 1


_COMPILER_PARAMS = pltpu.CompilerParams(
    dimension_semantics=("parallel", "parallel"),
    vmem_limit_bytes=VMEM_LIMIT_BYTES,
)


def _rms(x, gain, n):
    ms = jnp.sum(x * x, axis=-1, keepdims=True) * (1.0 / n)
    return x * lax.rsqrt(ms + EPS) * gain


def _const_spec(shape):
    zeros = (0,) * len(shape)
    return pl.BlockSpec(shape, lambda *_: zeros, pipeline_mode=pl.Buffered(1))


def _inproj_kernel(x_ref, g_ref, win_ref, qg_ref, kg_ref, cos_ref, sin_ref, hsum_ref, q_ref, k_ref, v_ref, u_ref):
    d = lax.broadcasted_iota(jnp.int32, (IN_SUB, LANES), 1) % HEAD_DIM
    scale = HEAD_DIM ** -0.5 * LOG2E

    def project(r0):
        h = _rms(x_ref[0, r0:r0 + IN_SUB], g_ref[...], D_MODEL).astype(BF16)
        return jnp.dot(h, win_ref[...], preferred_element_type=F32)

    def finish(r0, z):
        cos_t = cos_ref[r0:r0 + IN_SUB]
        sin_t = sin_ref[r0:r0 + IN_SUB]
        sin_lo = jnp.where(d < ROT_HALF, -sin_t, 0.0)
        sin_hi = jnp.where(d >= ROT_HALF, sin_t, 0.0)

        def norm_rope(t, gain):
            w = t.shape[1]
            wide = lambda a: jnp.concatenate([a] * (w // LANES), axis=1)
            ssq = jnp.dot((t * t).astype(BF16), hsum_ref[:w, :w], preferred_element_type=F32)
            tn = t * lax.rsqrt(ssq * (1.0 / HEAD_DIM) + EPS) * wide(gain)
            up = pltpu.roll(tn, w - ROT_HALF, 1)
            dn = pltpu.roll(tn, ROT_HALF, 1)
            return tn * wide(cos_t) + up * wide(sin_lo) + dn * wide(sin_hi)

        for t in range(ATTN_WIDTH // MXU_DIM):
            qt = norm_rope(z[:, t * MXU_DIM:(t + 1) * MXU_DIM], qg_ref[...])
            q_ref[0, r0:r0 + IN_SUB, t * MXU_DIM:(t + 1) * MXU_DIM] = (qt * scale).astype(BF16)
        k_ref[0, r0:r0 + IN_SUB] = norm_rope(z[:, ATTN_WIDTH:ATTN_WIDTH + KV_WIDTH], kg_ref[...]).astype(BF16)
        v_ref[0, r0:r0 + IN_SUB] = z[:, ATTN_WIDTH + KV_WIDTH:ATTN_WIDTH + 2 * KV_WIDTH].astype(BF16)
        for t in range(F_SLABS):
            lo = ATTN_WIDTH + 2 * KV_WIDTH + t * LANES
            u_ref[0, t, r0:r0 + IN_SUB] = z[:, lo:lo + LANES]

    starts = list(range(0, IN_TILE, IN_SUB))
    z = project(starts[0])
    for n, r0 in enumerate(starts):
        z_next = project(starts[n + 1]) if n + 1 < len(starts) else None
        finish(r0, z)
        z = z_next


def _inproj(x, attn_norm, w_in, q_gain, k_gain, cos_t, sin_t, hsum):
    nb, s, _ = x.shape
    t = IN_TILE
    tok = lambda w: pl.BlockSpec((1, t, w), lambda b, i: (b, i, 0))
    pos = pl.BlockSpec((t, LANES), lambda b, i: (i, 0))
    out_shape = (
        jax.ShapeDtypeStruct((nb, s, ATTN_WIDTH), BF16),
        jax.ShapeDtypeStruct((nb, s, KV_WIDTH), BF16),
        jax.ShapeDtypeStruct((nb, s, KV_WIDTH), BF16),
        jax.ShapeDtypeStruct((nb, F_SLABS, s, LANES), F32),
    )
    return pl.pallas_call(
        _inproj_kernel,
        grid=(nb, s // t),
        in_specs=[tok(D_MODEL), _const_spec((1, D_MODEL)), _const_spec((D_MODEL, IN_WIDTH)),
                  _const_spec((1, LANES)), _const_spec((1, LANES)), pos, pos, _const_spec((MXU_DIM, MXU_DIM))],
        out_specs=(tok(ATTN_WIDTH), tok(KV_WIDTH), tok(KV_WIDTH),
                   pl.BlockSpec((1, F_SLABS, t, LANES), lambda b, i: (b, 0, i, 0))),
        out_shape=out_shape,
        compiler_params=_COMPILER_PARAMS,
        name="inproj",
    )(x, attn_norm, w_in, q_gain, k_gain, cos_t, sin_t, hsum)


def _attn_kernel(sink_ref, q_ref, kp_ref, km_ref, kn_ref, vp_ref, vm_ref, vn_ref, g_ref, o_ref, *, seq):
    i = pl.program_id(1)
    kcat = jnp.concatenate([kp_ref[0], km_ref[0], kn_ref[0]], axis=0)
    vcat = jnp.concatenate([vp_ref[0], vm_ref[0], vn_ref[0]], axis=0)
    nkeys = 3 * BLOCK
    diff = (lax.broadcasted_iota(jnp.int32, (BLOCK, BLOCK), 1)
            - lax.broadcasted_iota(jnp.int32, (BLOCK, BLOCK), 0))
    lane = lax.broadcasted_iota(jnp.int32, (BLOCK, LANES), 1)
    low = lane < HEAD_DIM
    gain = g_ref[...]
    units = [(j, g) for j in range(ATTN_TILE // BLOCK) for g in range(N_KV_HEADS)]

    def scores(j, g):
        kw = kcat[j * BLOCK:j * BLOCK + nkeys]
        sel = low if g == 0 else jnp.logical_not(low)
        qs = []
        for hh in range(GQA_GROUP):
            qt = q_ref[0, j * BLOCK:(j + 1) * BLOCK, hh * LANES:(hh + 1) * LANES]
            qs.append(jnp.where(sel, qt, jnp.zeros_like(qt)))
        qg = jnp.concatenate(qs, axis=0)
        return lax.dot_general(qg, kw, (((1,), (1,)), ((), ())), preferred_element_type=F32)

    def softmax_pv(j, g, s):
        qblk = i * (ATTN_TILE // BLOCK) + j
        lo = jnp.where(qblk > 0, 0, BLOCK)
        hi = jnp.where(qblk < seq // BLOCK - 1, 0, -BLOCK)
        mask_prev = diff >= lo
        mask_next = diff <= hi
        vw = vcat[j * BLOCK:j * BLOCK + nkeys]
        ps, rs = [], []
        for hh in range(GQA_GROUP):
            sk = sink_ref[g * GQA_GROUP + hh] * LOG2E
            sh = s[hh * BLOCK:(hh + 1) * BLOCK]
            sh = jnp.concatenate([jnp.where(mask_prev, sh[:, :BLOCK], -jnp.inf), sh[:, BLOCK:2 * BLOCK],
                                  jnp.where(mask_next, sh[:, 2 * BLOCK:], -jnp.inf)], axis=1)
            m = jnp.maximum(jnp.max(sh, axis=-1, keepdims=True), sk)
            p = jnp.exp2(sh - m)
            denom = jnp.sum(p, axis=-1, keepdims=True) + jnp.exp2(sk - m)
            ps.append(p.astype(BF16))
            rs.append(1.0 / denom)
        p = jnp.concatenate(ps, axis=0)
        o = jnp.dot(p, vw, preferred_element_type=F32)
        return [o[hh * BLOCK:(hh + 1) * BLOCK] * rs[hh] for hh in range(GQA_GROUP)]

    pending = [scores(*u) for u in units[:ATTN_LOOKAHEAD]]
    outs = []
    for n, (j, g) in enumerate(units):
        if n + ATTN_LOOKAHEAD < len(units):
            pending.append(scores(*units[n + ATTN_LOOKAHEAD]))
        outs.append(softmax_pv(j, g, pending.pop(0)))
        if g == N_KV_HEADS - 1:
            a = jnp.concatenate([jnp.where(low, outs[0][hh], outs[1][hh]) for hh in range(GQA_GROUP)], axis=1)
            o_ref[0, j * BLOCK:(j + 1) * BLOCK, :] = _rms(a, gain, ATTN_WIDTH).astype(BF16)
            outs = []


def _attention(sink, q, k, v, out_gain):
    nb, s, _ = q.shape
    t = ATTN_TILE
    r = t // BLOCK
    last = s // BLOCK - 1
    main = lambda w: pl.BlockSpec((1, t, w), lambda b, i, *_: (b, i, 0))
    prev = pl.BlockSpec((1, BLOCK, KV_WIDTH), lambda b, i, *_: (b, jnp.maximum(i * r - 1, 0), 0))
    nxt = pl.BlockSpec((1, BLOCK, KV_WIDTH), lambda b, i, *_: (b, jnp.minimum((i + 1) * r, last), 0))
    grid_spec = pltpu.PrefetchScalarGridSpec(
        num_scalar_prefetch=1,
        grid=(nb, s // t),
        in_specs=[main(ATTN_WIDTH), prev, main(KV_WIDTH), nxt, prev, main(KV_WIDTH), nxt,
                  pl.BlockSpec((1, ATTN_WIDTH), lambda b, i, *_: (0, 0))],
        out_specs=main(ATTN_WIDTH),
    )
    return pl.pallas_call(
        functools.partial(_attn_kernel, seq=s),
        grid_spec=grid_spec,
        out_shape=jax.ShapeDtypeStruct((nb, s, ATTN_WIDTH), BF16),
        compiler_params=_COMPILER_PARAMS,
        name="attention",
    )(sink, q, k, k, k, v, v, v, out_gain)


def _dft1_kernel(m1_ref, cs_ref, u_ref, y_ref):
    m1 = m1_ref[...]
    cs = cs_ref[...]

    def channel_dft(j):
        slabs = []
        for t in range(F_SLABS):
            rows = u_ref.at[0, t].reshape(DFT_N * S1_GROUP, LANES)
            slabs.append(rows[pl.ds(j, DFT_N, stride=S1_GROUP), :].astype(BF16))
        per = MXU_DIM // LANES
        halves = [jnp.dot(jnp.concatenate(slabs[c * per:(c + 1) * per], axis=1), cs, preferred_element_type=F32)
                  for c in range(F_SLABS // per)]
        a = jnp.concatenate([h[:, :MXU_DIM] for h in halves], axis=1)
        b = jnp.concatenate([h[:, MXU_DIM:] for h in halves], axis=1)
        return jnp.concatenate([a, b], axis=0).astype(BF16)

    ab = [channel_dft(j) for j in range(S1_GROUP)]
    for j in range(S1_GROUP):
        y = jnp.dot(m1, ab[j], preferred_element_type=F32)
        y_ref[0, :, j * FOURIER_WIDTH:(j + 1) * FOURIER_WIDTH] = y.astype(BF16)


def _dft_stage1(m1, cs, u):
    nb = u.shape[0]
    groups = DFT_N // S1_GROUP
    u = u.reshape(nb, F_SLABS, DFT_N, groups, S1_GROUP, LANES)
    cols = S1_GROUP * FOURIER_WIDTH
    return pl.pallas_call(
        _dft1_kernel,
        grid=(nb, groups),
        in_specs=[_const_spec((2 * DFT_N, 2 * DFT_N)), _const_spec((MXU_DIM, 2 * MXU_DIM)),
                  pl.BlockSpec((1, F_SLABS, DFT_N, 1, S1_GROUP, LANES), lambda n, g: (n, 0, 0, g, 0, 0))],
        out_specs=pl.BlockSpec((1, 2 * DFT_N, cols), lambda n, g: (n, 0, g)),
        out_shape=jax.ShapeDtypeStruct((nb, 2 * DFT_N, DFT_N * FOURIER_WIDTH), BF16),
        compiler_params=_COMPILER_PARAMS,
        name="dft_stage1",
    )(m1, cs, u)


def _dft2_kernel(cn_ref, sn_ref, wc_ref, ws_ref, y_ref, wf_ref, g_ref, o_ref):
    cn = cn_ref[...]
    sn = sn_ref[...]
    wf = wf_ref[...]
    gain = g_ref[...]

    def seq_dft(j):
        wc = wc_ref[j:j + 1, :]
        ws = ws_ref[j:j + 1, :]
        gc = cn * wc - sn * ws
        gs = sn * wc + cn * ws
        g = jnp.concatenate([gc, gs], axis=1).astype(BF16)
        y = jnp.concatenate([y_ref[0, 0, j], y_ref[0, 1, j]], axis=0)
        return jnp.dot(g, y, preferred_element_type=F32).astype(BF16)

    xr = [seq_dft(j) for j in range(S2_K1)]
    fs = [jnp.dot(xr[j], wf, preferred_element_type=F32) for j in range(S2_K1)]
    for j in range(S2_K1):
        fn = _rms(fs[j], gain, FOURIER_WIDTH)
        for t in range(F_SLABS):
            rows = o_ref.at[0, t].reshape(DFT_N * S2_K1, LANES)
            rows[pl.ds(j, DFT_N, stride=S2_K1), :] = fn[:, t * LANES:(t + 1) * LANES]


def _dft_stage2(cn, sn, wc, ws, y, wf_bd, out_gain):
    nb = y.shape[0]
    y = y.reshape(nb, 2, DFT_N, DFT_N, FOURIER_WIDTH)
    groups = DFT_N // S2_K1
    tw = pl.BlockSpec((S2_K1, DFT_N), lambda n, k: (k, 0))
    out = pl.pallas_call(
        _dft2_kernel,
        grid=(nb, DFT_N // S2_K1),
        in_specs=[_const_spec((DFT_N, DFT_N)), _const_spec((DFT_N, DFT_N)), tw, tw,
                  pl.BlockSpec((1, 2, S2_K1, DFT_N, FOURIER_WIDTH), lambda n, k: (n, 0, k, 0, 0)),
                  _const_spec((FOURIER_WIDTH, FOURIER_WIDTH)), _const_spec((1, FOURIER_WIDTH))],
        out_specs=pl.BlockSpec((1, F_SLABS, DFT_N, 1, S2_K1, LANES), lambda n, k: (n, 0, 0, k, 0, 0)),
        out_shape=jax.ShapeDtypeStruct((nb, F_SLABS, DFT_N, groups, S2_K1, LANES), F32),
        compiler_params=_COMPILER_PARAMS,
        name="dft_stage2",
    )(cn, sn, wc, ws, y, wf_bd, out_gain)
    return out.reshape(nb, F_SLABS, DFT_N * DFT_N, LANES)


def _ffn_kernel(xm_ref, xp_ref, xn_ref, am_ref, ap_ref, an_ref, fm_ref, fp_ref, fn_ref, p_ref,
                wout_ref, gffn_ref, wup_ref, cw_ref, cb_ref, wdown_ref, wple_ref, gple_ref, wgate_ref,
                bgate_ref, o_ref, hu_scr, nat_scr):
    i = pl.program_id(1)
    nt = pl.num_programs(1)
    t = FFN_TILE
    half = t // 2
    ext = t + 2 * FFN_HALO
    trim = FFN_HALO - UP_HALO
    up_ext = t + 2 * UP_HALO
    x_all = jnp.concatenate([xp_ref[0], xm_ref[0], xn_ref[0]], axis=0)
    a_all = jnp.concatenate([ap_ref[0], am_ref[0], an_ref[0]], axis=0)
    f_all = [jnp.concatenate([fp_ref[0, s], fm_ref[0, s], fn_ref[0, s]], axis=0).astype(BF16)
             for s in range(F_SLABS)]
    mixed_all = jnp.concatenate([a_all] + f_all, axis=1)
    row = lax.broadcasted_iota(jnp.int32, (ext, 1), 0)

    def prologue(s):
        r0 = s * t
        x1 = x_all[r0:r0 + ext] + jnp.dot(mixed_all[r0:r0 + ext], wout_ref[...], preferred_element_type=F32)
        e = jnp.dot(p_ref[0, r0:r0 + t].astype(BF16), wple_ref[...], preferred_element_type=F32)
        e = _rms(e, gple_ref[...], D_MODEL)
        h = _rms(x1, gffn_ref[...], D_MODEL)
        first_valid = jnp.where(i == 0, FFN_HALO, 0) if s == 0 else 0
        end_valid = jnp.where(i == nt - 1, t + FFN_HALO, ext) if s == FFN_SUBTILES - 1 else ext
        h = jnp.where((row >= first_valid) & (row < end_valid), h, 0.0)
        return x1[FFN_HALO:FFN_HALO + t], e, h[trim:trim + up_ext].astype(BF16)

    def up_proj(h, j, slot):
        for part, c0 in enumerate((j * FFN_CHUNK, D_FF + j * FFN_CHUNK)):
            hu = jnp.dot(h, wup_ref[:, c0:c0 + FFN_CHUNK], preferred_element_type=F32)
            for c in range(FFN_CHUNK // LANES):
                hu_scr[slot, part, c] = hu[:, c * LANES:(c + 1) * LANES]

    def conv(slot, part, c0):
        cols = []
        for c in range(FFN_CHUNK // LANES):
            cw = cw_ref[:, c0 + c * LANES:c0 + (c + 1) * LANES]
            cb = cb_ref[:, c0 + c * LANES:c0 + (c + 1) * LANES]
            rows = [hu_scr[slot, part, c, pl.ds(UP_HALO - 1 + k, half, stride=2), :] for k in range(4)]
            even = rows[0] * cw[0:1] + rows[1] * cw[1:2] + rows[2] * cw[2:3] + cb
            odd = rows[1] * cw[0:1] + rows[2] * cw[1:2] + rows[3] * cw[2:3] + cb
            cols.append(jnp.concatenate([even, odd], axis=0))
        return jnp.concatenate(cols, axis=1)

    def epilogue(s, x1, e, acc):
        for c in range(D_MODEL // LANES):
            nat_scr[c, pl.ds(0, half, stride=2), :] = acc[:half, c * LANES:(c + 1) * LANES]
            nat_scr[c, pl.ds(1, half, stride=2), :] = acc[half:, c * LANES:(c + 1) * LANES]
        x2 = x1 + jnp.concatenate([nat_scr[c] for c in range(D_MODEL // LANES)], axis=1)
        gate = jax.nn.sigmoid(jnp.dot(x2.astype(BF16), wgate_ref[...], preferred_element_type=F32)
                              + bgate_ref[...])
        o_ref[0, s * t:(s + 1) * t, :] = x2 + gate * e

    units = [(s, j) for s in range(FFN_SUBTILES) for j in range(N_FFN_CHUNKS)]
    state = {}

    def issue(n):
        s, j = units[n]
        if j == 0:
            state[s] = list(prologue(s)) + [jnp.zeros((t, D_MODEL), F32)]
        up_proj(state[s][2], j, n % HU_SLOTS)

    for n in range(FFN_LOOKAHEAD):
        issue(n)
    for n, (s, j) in enumerate(units):
        if n + FFN_LOOKAHEAD < len(units):
            issue(n + FFN_LOOKAHEAD)
        gate = conv(n % HU_SLOTS, 0, j * FFN_CHUNK)
        up = conv(n % HU_SLOTS, 1, D_FF + j * FFN_CHUNK)
        act = (jax.nn.silu(gate) * up).astype(BF16)
        state[s][3] = state[s][3] + jnp.dot(act, wdown_ref[j * FFN_CHUNK:(j + 1) * FFN_CHUNK, :],
                                            preferred_element_type=F32)
        if j == N_FFN_CHUNKS - 1:
            x1, e, _, acc = state.pop(s)
            epilogue(s, x1, e, acc)


def _ffn(x, a, f, p, w_out, ffn_gain, w_up, conv_w, conv_b, w_down, w_ple, ple_gain, w_gate, b_gate):
    nb, s, _ = x.shape
    t = FFN_SUBTILES * FFN_TILE
    r = t // FFN_HALO
    last = s // FFN_HALO - 1

    def specs(w):
        return [pl.BlockSpec((1, t, w), lambda b, i: (b, i, 0)),
                pl.BlockSpec((1, FFN_HALO, w), lambda b, i: (b, jnp.maximum(i * r - 1, 0), 0)),
                pl.BlockSpec((1, FFN_HALO, w), lambda b, i: (b, jnp.minimum((i + 1) * r, last), 0))]

    f_specs = [pl.BlockSpec((1, F_SLABS, t, LANES), lambda b, i: (b, 0, i, 0)),
               pl.BlockSpec((1, F_SLABS, FFN_HALO, LANES), lambda b, i: (b, 0, jnp.maximum(i * r - 1, 0), 0)),
               pl.BlockSpec((1, F_SLABS, FFN_HALO, LANES), lambda b, i: (b, 0, jnp.minimum((i + 1) * r, last), 0))]

    in_specs = (specs(D_MODEL) + specs(ATTN_WIDTH) + f_specs
                + [pl.BlockSpec((1, t, PLE_DIM), lambda b, i: (b, i, 0)),
                   _const_spec((D_MODEL, D_MODEL)), _const_spec((1, D_MODEL)),
                   _const_spec((D_MODEL, 2 * D_FF)), _const_spec((3, 2 * D_FF)), _const_spec((1, 2 * D_FF)),
                   _const_spec((D_FF, D_MODEL)), _const_spec((PLE_DIM, D_MODEL)), _const_spec((1, D_MODEL)),
                   _const_spec((D_MODEL, D_MODEL)), _const_spec((1, D_MODEL))])
    return pl.pallas_call(
        _ffn_kernel,
        grid=(nb, s // t),
        in_specs=in_specs,
        out_specs=pl.BlockSpec((1, t, D_MODEL), lambda b, i: (b, i, 0)),
        out_shape=jax.ShapeDtypeStruct((nb, s, D_MODEL), F32),
        scratch_shapes=[pltpu.VMEM((HU_SLOTS, 2, FFN_CHUNK // LANES, FFN_TILE + 2 * UP_HALO, LANES), F32),
                        pltpu.VMEM((D_MODEL // LANES, FFN_TILE, LANES), F32)],
        compiler_params=_COMPILER_PARAMS,
        name="ffn",
    )(x, x, x, a, a, a, f, f, f, p, w_out, ffn_gain, w_up, conv_w, conv_b, w_down, w_ple, ple_gain,
      w_gate, b_gate)


def _dft_tables():
    n = np.arange(DFT_N)
    ang = 2.0 * np.pi * ((n[:, None] * n[None, :]) % DFT_N) / DFT_N
    cn, sn = np.cos(ang), np.sin(ang)
    m1 = np.block([[cn, -sn], [-sn, -cn]])
    seq = DFT_N * DFT_N
    tw = 2.0 * np.pi * ((n[:, None] * n[None, :]) % seq) / seq
    ortho = 1.0 / np.sqrt(seq * FOURIER_GROUP_DIM)
    c = np.arange(FOURIER_GROUP_DIM)
    ang_c = 2.0 * np.pi * ((c[:, None] * c[None, :]) % FOURIER_GROUP_DIM) / FOURIER_GROUP_DIM
    eye = np.eye(MXU_DIM // FOURIER_GROUP_DIM)
    cs = np.concatenate([np.kron(eye, np.cos(ang_c)), np.kron(eye, np.sin(ang_c))], axis=1)
    hsum = np.kron(np.eye(MXU_DIM // HEAD_DIM), np.ones((HEAD_DIM, HEAD_DIM)))
    f32 = lambda a: jnp.asarray(a, dtype=F32)
    return (f32(m1), f32(cn * ortho), f32(sn * ortho), f32(np.cos(tw)), f32(np.sin(tw)), f32(cs), f32(hsum))


def _rope_tables(seq):
    inv_freq = ROPE_THETA ** (-jnp.arange(0, ROT_DIM, 2, dtype=F32) / ROT_DIM)
    inv_head = jnp.concatenate([inv_freq, inv_freq, jnp.zeros((HEAD_DIM - ROT_DIM,), F32)])
    inv_lane = jnp.concatenate([inv_head] * (LANES // HEAD_DIM))
    ang = jnp.arange(seq, dtype=F32)[:, None] * inv_lane[None, :]
    return jnp.cos(ang), jnp.sin(ang)


_HEAD_ORDER = tuple(g * GQA_GROUP + i for i in range(GQA_GROUP) for g in range(N_KV_HEADS))


def _permute_heads(a, axis):
    blocks = [lax.slice_in_dim(a, h * HEAD_DIM, (h + 1) * HEAD_DIM, axis=axis) for h in _HEAD_ORDER]
    return jnp.concatenate(blocks, axis=axis)


def _block_diag(w):
    g, c, e = w.shape
    eye = jnp.eye(g, dtype=w.dtype)
    return (eye[:, None, :, None] * w[:, :, None, :]).reshape(g * c, g * e)


def _layer(xs, ps, prm, tables):
    (attn_norm, w_in, q_norm, k_norm, sink, w_fourier, attn_out_norm, fourier_out_norm, w_out, ffn_norm,
     w_up, conv_w, conv_b, w_down, w_ple, ple_norm, w_ple_gate, b_ple_gate) = prm
    m1, cn, sn, wc, ws, cs_bd, hsum = tables
    row = lambda v: v.reshape(1, -1)
    two = lambda v: jnp.concatenate([v] * (LANES // HEAD_DIM)).reshape(1, LANES)

    w_in_p = jnp.concatenate([_permute_heads(w_in[:, :ATTN_WIDTH], 1), w_in[:, ATTN_WIDTH:]], axis=1)
    w_out_p = jnp.concatenate([_permute_heads(w_out[:ATTN_WIDTH], 0), w_out[ATTN_WIDTH:]], axis=0)
    proj = []
    for x in xs:
        assert x.shape[1] == DFT_N * DFT_N
        cos_t, sin_t = _rope_tables(x.shape[1])
        proj.append(_inproj(x, row(attn_norm), w_in_p.astype(BF16), two(q_norm), two(k_norm), cos_t, sin_t,
                            hsum.astype(BF16)))
    ys = [_dft_stage1(m1.astype(BF16), cs_bd.astype(BF16), u) for _, _, _, u in proj]
    attns = [_attention(sink, q, k, v, row(_permute_heads(attn_out_norm, 0))) for q, k, v, _ in proj]
    fours = [_dft_stage2(cn, sn, wc, ws, y, _block_diag(w_fourier).astype(BF16), row(fourier_out_norm))
             for y in ys]
    return [_ffn(x, attn, four, p, w_out_p.astype(BF16), row(ffn_norm), w_up.astype(BF16), conv_w,
                 row(conv_b), w_down.astype(BF16), w_ple.astype(BF16), row(ple_norm),
                 w_ple_gate.astype(BF16), row(b_ple_gate))
            for x, attn, four, p in zip(xs, attns, fours, ps)]


def kernel(x_prompt, x_sample, p_prompt, p_sample, attn_norm, w_in, q_norm, k_norm, sink, w_fourier,
           attn_out_norm, fourier_out_norm, w_out, ffn_norm, w_up, conv_w, conv_b, w_down, w_ple, ple_norm,
           w_ple_gate, b_ple_gate):
    stacked = (attn_norm, w_in, q_norm, k_norm, sink, w_fourier, attn_out_norm, fourier_out_norm, w_out,
               ffn_norm, w_up, conv_w, conv_b, w_down, w_ple, ple_norm, w_ple_gate, b_ple_gate)
    tables = _dft_tables()
    y_prompt, y_sample = x_prompt, x_sample
    for i in range(attn_norm.shape[0]):
        prm = tuple(w[i] for w in stacked)
        y_prompt, y_sample = _layer([y_prompt, y_sample], [p_prompt[i], p_sample[i]], prm, tables)
    return (y_prompt, y_sample)
```

```python
import functools

import numpy as np
import jax
import jax.numpy as jnp
from jax import lax
from jax.experimental import pallas as pl
from jax.experimental.pallas import tpu as pltpu

F32 = jnp.float32
BF16 = jnp.bfloat16

D_MODEL = 1024
HEAD_DIM = 64
N_HEADS = 8
N_KV_HEADS = 2
GQA_GROUP = N_HEADS // N_KV_HEADS
ATTN_WIDTH = N_HEADS * HEAD_DIM
KV_WIDTH = N_KV_HEADS * HEAD_DIM
FOURIER_WIDTH = 512
FOURIER_GROUP_DIM = 64
N_FOURIER_GROUPS = FOURIER_WIDTH // FOURIER_GROUP_DIM
IN_WIDTH = ATTN_WIDTH + 2 * KV_WIDTH + FOURIER_WIDTH
BLOCK = 128
ROPE_THETA = 500000.0
ROT_DIM = HEAD_DIM // 4
ROT_HALF = ROT_DIM // 2
D_FF = 2816
PLE_DIM = 256
EPS = 1e-6
LOG2E = 1.4426950408889634

LANES = 128
MXU_DIM = 256
BF16_ROWS = 16
VMEM_LIMIT_BYTES = 56 * 1024 * 1024

DFT_N = 128

IN_TILE = 1024
IN_SUB = 128
ATTN_TILE = 1024
ATTN_LOOKAHEAD = 2
FFN_LOOKAHEAD = 2
F32_ROWS = 8
S1_GROUP = F32_ROWS
F_SLABS = FOURIER_WIDTH // LANES
S2_K1 = F32_ROWS
FFN_TILE = 512
FFN_SUBTILES = 1
FFN_HALO = BF16_ROWS
UP_HALO = F32_ROWS
FFN_CHUNK = 256
N_FFN_CHUNKS = D_FF // FFN_CHUNK
HU_SLOTS = FFN_LOOKAHEAD + 1


_COMPILER_PARAMS = pltpu.CompilerParams(
    dimension_semantics=("parallel", "parallel"),
    vmem_limit_bytes=VMEM_LIMIT_BYTES,
)


def _rms(x, gain, n):
    ms = jnp.sum(x * x, axis=-1, keepdims=True) * (1.0 / n)
    return x * lax.rsqrt(ms + EPS) * gain


def _const_spec(shape):
    zeros = (0,) * len(shape)
    return pl.BlockSpec(shape, lambda *_: zeros, pipeline_mode=pl.Buffered(1))


def _inproj_kernel(x_ref, g_ref, win_ref, qg_ref, kg_ref, cos_ref, sin_ref, hsum_ref, q_ref, k_ref, v_ref, u_ref):
    d = lax.broadcasted_iota(jnp.int32, (IN_SUB, LANES), 1) % HEAD_DIM
    scale = HEAD_DIM ** -0.5 * LOG2E

    def project(r0):
        h = _rms(x_ref[0, r0:r0 + IN_SUB], g_ref[...], D_MODEL).astype(BF16)
        return jnp.dot(h, win_ref[...], preferred_element_type=F32)

    def finish(r0, z):
        cos_t = cos_ref[r0:r0 + IN_SUB]
        sin_t = sin_ref[r0:r0 + IN_SUB]
        sin_lo = jnp.where(d < ROT_HALF, -sin_t, 0.0)
        sin_hi = jnp.where(d >= ROT_HALF, sin_t, 0.0)

        def norm_rope(t, gain):
            w = t.shape[1]
            wide = lambda a: jnp.concatenate([a] * (w // LANES), axis=1)
            ssq = jnp.dot((t * t).astype(BF16), hsum_ref[:w, :w], preferred_element_type=F32)
            tn = t * lax.rsqrt(ssq * (1.0 / HEAD_DIM) + EPS) * wide(gain)
            up = pltpu.roll(tn, w - ROT_HALF, 1)
            dn = pltpu.roll(tn, ROT_HALF, 1)
            return tn * wide(cos_t) + up * wide(sin_lo) + dn * wide(sin_hi)

        for t in range(ATTN_WIDTH // MXU_DIM):
            qt = norm_rope(z[:, t * MXU_DIM:(t + 1) * MXU_DIM], qg_ref[...])
            q_ref[0, r0:r0 + IN_SUB, t * MXU_DIM:(t + 1) * MXU_DIM] = (qt * scale).astype(BF16)
        k_ref[0, r0:r0 + IN_SUB] = norm_rope(z[:, ATTN_WIDTH:ATTN_WIDTH + KV_WIDTH], kg_ref[...]).astype(BF16)
        v_ref[0, r0:r0 + IN_SUB] = z[:, ATTN_WIDTH + KV_WIDTH:ATTN_WIDTH + 2 * KV_WIDTH].astype(BF16)
        for t in range(F_SLABS):
            lo = ATTN_WIDTH + 2 * KV_WIDTH + t * LANES
            u_ref[0, t, r0:r0 + IN_SUB] = z[:, lo:lo + LANES]

    starts = list(range(0, IN_TILE, IN_SUB))
    z = project(starts[0])
    for n, r0 in enumerate(starts):
        z_next = project(starts[n + 1]) if n + 1 < len(starts) else None
        finish(r0, z)
        z = z_next


def _inproj(x, attn_norm, w_in, q_gain, k_gain, cos_t, sin_t, hsum):
    nb, s, _ = x.shape
    t = IN_TILE
    tok = lambda w: pl.BlockSpec((1, t, w), lambda b, i: (b, i, 0))
    pos = pl.BlockSpec((t, LANES), lambda b, i: (i, 0))
    out_shape = (
        jax.ShapeDtypeStruct((nb, s, ATTN_WIDTH), BF16),
        jax.ShapeDtypeStruct((nb, s, KV_WIDTH), BF16),
        jax.ShapeDtypeStruct((nb, s, KV_WIDTH), BF16),
        jax.ShapeDtypeStruct((nb, F_SLABS, s, LANES), F32),
    )
    return pl.pallas_call(
        _inproj_kernel,
        grid=(nb, s // t),
        in_specs=[tok(D_MODEL), _const_spec((1, D_MODEL)), _const_spec((D_MODEL, IN_WIDTH)),
                  _const_spec((1, LANES)), _const_spec((1, LANES)), pos, pos, _const_spec((MXU_DIM, MXU_DIM))],
        out_specs=(tok(ATTN_WIDTH), tok(KV_WIDTH), tok(KV_WIDTH),
                   pl.BlockSpec((1, F_SLABS, t, LANES), lambda b, i: (b, 0, i, 0))),
        out_shape=out_shape,
        compiler_params=_COMPILER_PARAMS,
        name="inproj",
    )(x, attn_norm, w_in, q_gain, k_gain, cos_t, sin_t, hsum)


def _attn_kernel(sink_ref, q_ref, kp_ref, km_ref, kn_ref, vp_ref, vm_ref, vn_ref, g_ref, o_ref, *, seq):
    i = pl.program_id(1)
    kcat = jnp.concatenate([kp_ref[0], km_ref[0], kn_ref[0]], axis=0)
    vcat = jnp.concatenate([vp_ref[0], vm_ref[0], vn_ref[0]], axis=0)
    nkeys = 3 * BLOCK
    diff = (lax.broadcasted_iota(jnp.int32, (BLOCK, BLOCK), 1)
            - lax.broadcasted_iota(jnp.int32, (BLOCK, BLOCK), 0))
    lane = lax.broadcasted_iota(jnp.int32, (BLOCK, LANES), 1)
    low = lane < HEAD_DIM
    gain = g_ref[...]
    units = [(j, g) for j in range(ATTN_TILE // BLOCK) for g in range(N_KV_HEADS)]

    def scores(j, g):
        kw = kcat[j * BLOCK:j * BLOCK + nkeys]
        sel = low if g == 0 else jnp.logical_not(low)
        qs = []
        for hh in range(GQA_GROUP):
            qt = q_ref[0, j * BLOCK:(j + 1) * BLOCK, hh * LANES:(hh + 1) * LANES]
            qs.append(jnp.where(sel, qt, jnp.zeros_like(qt)))
        qg = jnp.concatenate(qs, axis=0)
        return lax.dot_general(qg, kw, (((1,), (1,)), ((), ())), preferred_element_type=F32)

    def softmax_pv(j, g, s):
        qblk = i * (ATTN_TILE // BLOCK) + j
        lo = jnp.where(qblk > 0, 0, BLOCK)
        hi = jnp.where(qblk < seq // BLOCK - 1, 0, -BLOCK)
        mask_prev = diff >= lo
        mask_next = diff <= hi
        vw = vcat[j * BLOCK:j * BLOCK + nkeys]
        ps, rs = [], []
        for hh in range(GQA_GROUP):
            sk = sink_ref[g * GQA_GROUP + hh] * LOG2E
            sh = s[hh * BLOCK:(hh + 1) * BLOCK]
            sh = jnp.concatenate([jnp.where(mask_prev, sh[:, :BLOCK], -jnp.inf), sh[:, BLOCK:2 * BLOCK],
                                  jnp.where(mask_next, sh[:, 2 * BLOCK:], -jnp.inf)], axis=1)
            m = jnp.maximum(jnp.max(sh, axis=-1, keepdims=True), sk)
            p = jnp.exp2(sh - m)
            denom = jnp.sum(p, axis=-1, keepdims=True) + jnp.exp2(sk - m)
            ps.append(p.astype(BF16))
            rs.append(1.0 / denom)
        p = jnp.concatenate(ps, axis=0)
        o = jnp.dot(p, vw, preferred_element_type=F32)
        return [o[hh * BLOCK:(hh + 1) * BLOCK] * rs[hh] for hh in range(GQA_GROUP)]

    pending = [scores(*u) for u in units[:ATTN_LOOKAHEAD]]
    outs = []
    for n, (j, g) in enumerate(units):
        if n + ATTN_LOOKAHEAD < len(units):
            pending.append(scores(*units[n + ATTN_LOOKAHEAD]))
        outs.append(softmax_pv(j, g, pending.pop(0)))
        if g == N_KV_HEADS - 1:
            a = jnp.concatenate([jnp.where(low, outs[0][hh], outs[1][hh]) for hh in range(GQA_GROUP)], axis=1)
            o_ref[0, j * BLOCK:(j + 1) * BLOCK, :] = _rms(a, gain, ATTN_WIDTH).astype(BF16)
            outs = []


def _attention(sink, q, k, v, out_gain):
    nb, s, _ = q.shape
    t = ATTN_TILE
    r = t // BLOCK
    last = s // BLOCK - 1
    main = lambda w: pl.BlockSpec((1, t, w), lambda b, i, *_: (b, i, 0))
    prev = pl.BlockSpec((1, BLOCK, KV_WIDTH), lambda b, i, *_: (b, jnp.maximum(i * r - 1, 0), 0))
    nxt = pl.BlockSpec((1, BLOCK, KV_WIDTH), lambda b, i, *_: (b, jnp.minimum((i + 1) * r, last), 0))
    grid_spec = pltpu.PrefetchScalarGridSpec(
        num_scalar_prefetch=1,
        grid=(nb, s // t),
        in_specs=[main(ATTN_WIDTH), prev, main(KV_WIDTH), nxt, prev, main(KV_WIDTH), nxt,
                  pl.BlockSpec((1, ATTN_WIDTH), lambda b, i, *_: (0, 0))],
        out_specs=main(ATTN_WIDTH),
    )
    return pl.pallas_call(
        functools.partial(_attn_kernel, seq=s),
        grid_spec=grid_spec,
        out_shape=jax.ShapeDtypeStruct((nb, s, ATTN_WIDTH), BF16),
        compiler_params=_COMPILER_PARAMS,
        name="attention",
    )(sink, q, k, k, k, v, v, v, out_gain)


def _dft1_kernel(m1_ref, cs_ref, u_ref, y_ref):
    m1 = m1_ref[...]
    cs = cs_ref[...]

    def channel_dft(j):
        slabs = []
        for t in range(F_SLABS):
            rows = u_ref.at[0, t].reshape(DFT_N * S1_GROUP, LANES)
            slabs.append(rows[pl.ds(j, DFT_N, stride=S1_GROUP), :].astype(BF16))
        per = MXU_DIM // LANES
        halves = [jnp.dot(jnp.concatenate(slabs[c * per:(c + 1) * per], axis=1), cs, preferred_element_type=F32)
                  for c in range(F_SLABS // per)]
        a = jnp.concatenate([h[:, :MXU_DIM] for h in halves], axis=1)
        b = jnp.concatenate([h[:, MXU_DIM:] for h in halves], axis=1)
        return jnp.concatenate([a, b], axis=0).astype(BF16)

    ab = [channel_dft(j) for j in range(S1_GROUP)]
    for j in range(S1_GROUP):
        y = jnp.dot(m1, ab[j], preferred_element_type=F32)
        y_ref[0, :, j * FOURIER_WIDTH:(j + 1) * FOURIER_WIDTH] = y.astype(BF16)


def _dft_stage1(m1, cs, u):
    nb = u.shape[0]
    groups = DFT_N // S1_GROUP
    u = u.reshape(nb, F_SLABS, DFT_N, groups, S1_GROUP, LANES)
    cols = S1_GROUP * FOURIER_WIDTH
    return pl.pallas_call(
        _dft1_kernel,
        grid=(nb, groups),
        in_specs=[_const_spec((2 * DFT_N, 2 * DFT_N)), _const_spec((MXU_DIM, 2 * MXU_DIM)),
                  pl.BlockSpec((1, F_SLABS, DFT_N, 1, S1_GROUP, LANES), lambda n, g: (n, 0, 0, g, 0, 0))],
        out_specs=pl.BlockSpec((1, 2 * DFT_N, cols), lambda n, g: (n, 0, g)),
        out_shape=jax.ShapeDtypeStruct((nb, 2 * DFT_N, DFT_N * FOURIER_WIDTH), BF16),
        compiler_params=_COMPILER_PARAMS,
        name="dft_stage1",
    )(m1, cs, u)


def _dft2_kernel(cn_ref, sn_ref, wc_ref, ws_ref, y_ref, wf_ref, g_ref, o_ref):
    cn = cn_ref[...]
    sn = sn_ref[...]
    wf = wf_ref[...]
    gain = g_ref[...]

    def seq_dft(j):
        wc = wc_ref[j:j + 1, :]
        ws = ws_ref[j:j + 1, :]
        gc = cn * wc - sn * ws
        gs = sn * wc + cn * ws
        g = jnp.concatenate([gc, gs], axis=1).astype(BF16)
        y = jnp.concatenate([y_ref[0, 0, j], y_ref[0, 1, j]], axis=0)
        return jnp.dot(g, y, preferred_element_type=F32).astype(BF16)

    xr = [seq_dft(j) for j in range(S2_K1)]
    fs = [jnp.dot(xr[j], wf, preferred_element_type=F32) for j in range(S2_K1)]
    for j in range(S2_K1):
        fn = _rms(fs[j], gain, FOURIER_WIDTH)
        for t in range(F_SLABS):
            rows = o_ref.at[0, t].reshape(DFT_N * S2_K1, LANES)
            rows[pl.ds(j, DFT_N, stride=S2_K1), :] = fn[:, t * LANES:(t + 1) * LANES]


def _dft_stage2(cn, sn, wc, ws, y, wf_bd, out_gain):
    nb = y.shape[0]
    y = y.reshape(nb, 2, DFT_N, DFT_N, FOURIER_WIDTH)
    groups = DFT_N // S2_K1
    tw = pl.BlockSpec((S2_K1, DFT_N), lambda n, k: (k, 0))
    out = pl.pallas_call(
        _dft2_kernel,
        grid=(nb, DFT_N // S2_K1),
        in_specs=[_const_spec((DFT_N, DFT_N)), _const_spec((DFT_N, DFT_N)), tw, tw,
                  pl.BlockSpec((1, 2, S2_K1, DFT_N, FOURIER_WIDTH), lambda n, k: (n, 0, k, 0, 0)),
                  _const_spec((FOURIER_WIDTH, FOURIER_WIDTH)), _const_spec((1, FOURIER_WIDTH))],
        out_specs=pl.BlockSpec((1, F_SLABS, DFT_N, 1, S2_K1, LANES), lambda n, k: (n, 0, 0, k, 0, 0)),
        out_shape=jax.ShapeDtypeStruct((nb, F_SLABS, DFT_N, groups, S2_K1, LANES), F32),
        compiler_params=_COMPILER_PARAMS,
        name="dft_stage2",
    )(cn, sn, wc, ws, y, wf_bd, out_gain)
    return out.reshape(nb, F_SLABS, DFT_N * DFT_N, LANES)


def _ffn_kernel(xm_ref, xp_ref, xn_ref, am_ref, ap_ref, an_ref, fm_ref, fp_ref, fn_ref, p_ref,
                wout_ref, gffn_ref, wup_ref, cw_ref, cb_ref, wdown_ref, wple_ref, gple_ref, wgate_ref,
                bgate_ref, o_ref, hu_scr, nat_scr):
    i = pl.program_id(1)
    nt = pl.num_programs(1)
    t = FFN_TILE
    half = t // 2
    ext = t + 2 * FFN_HALO
    trim = FFN_HALO - UP_HALO
    up_ext = t + 2 * UP_HALO
    x_all = jnp.concatenate([xp_ref[0], xm_ref[0], xn_ref[0]], axis=0)
    a_all = jnp.concatenate([ap_ref[0], am_ref[0], an_ref[0]], axis=0)
    f_all = [jnp.concatenate([fp_ref[0, s], fm_ref[0, s], fn_ref[0, s]], axis=0).astype(BF16)
             for s in range(F_SLABS)]
    mixed_all = jnp.concatenate([a_all] + f_all, axis=1)
    row = lax.broadcasted_iota(jnp.int32, (ext, 1), 0)

    def prologue(s):
        r0 = s * t
        x1 = x_all[r0:r0 + ext] + jnp.dot(mixed_all[r0:r0 + ext], wout_ref[...], preferred_element_type=F32)
        e = jnp.dot(p_ref[0, r0:r0 + t].astype(BF16), wple_ref[...], preferred_element_type=F32)
        e = _rms(e, gple_ref[...], D_MODEL)
        h = _rms(x1, gffn_ref[...], D_MODEL)
        first_valid = jnp.where(i == 0, FFN_HALO, 0) if s == 0 else 0
        end_valid = jnp.where(i == nt - 1, t + FFN_HALO, ext) if s == FFN_SUBTILES - 1 else ext
        h = jnp.where((row >= first_valid) & (row < end_valid), h, 0.0)
        return x1[FFN_HALO:FFN_HALO + t], e, h[trim:trim + up_ext].astype(BF16)

    def up_proj(h, j, slot):
        for part, c0 in enumerate((j * FFN_CHUNK, D_FF + j * FFN_CHUNK)):
            hu = jnp.dot(h, wup_ref[:, c0:c0 + FFN_CHUNK], preferred_element_type=F32)
            for c in range(FFN_CHUNK // LANES):
                hu_scr[slot, part, c] = hu[:, c * LANES:(c + 1) * LANES]

    def conv(slot, part, c0):
        cols = []
        for c in range(FFN_CHUNK // LANES):
            cw = cw_ref[:, c0 + c * LANES:c0 + (c + 1) * LANES]
            cb = cb_ref[:, c0 + c * LANES:c0 + (c + 1) * LANES]
            rows = [hu_scr[slot, part, c, pl.ds(UP_HALO - 1 + k, half, stride=2), :] for k in range(4)]
            even = rows[0] * cw[0:1] + rows[1] * cw[1:2] + rows[2] * cw[2:3] + cb
            odd = rows[1] * cw[0:1] + rows[2] * cw[1:2] + rows[3] * cw[2:3] + cb
            cols.append(jnp.concatenate([even, odd], axis=0))
        return jnp.concatenate(cols, axis=1)

    def epilogue(s, x1, e, acc):
        for c in range(D_MODEL // LANES):
            nat_scr[c, pl.ds(0, half, stride=2), :] = acc[:half, c * LANES:(c + 1) * LANES]
            nat_scr[c, pl.ds(1, half, stride=2), :] = acc[half:, c * LANES:(c + 1) * LANES]
        x2 = x1 + jnp.concatenate([nat_scr[c] for c in range(D_MODEL // LANES)], axis=1)
        gate = jax.nn.sigmoid(jnp.dot(x2.astype(BF16), wgate_ref[...], preferred_element_type=F32)
                              + bgate_ref[...])
        o_ref[0, s * t:(s + 1) * t, :] = x2 + gate * e

    units = [(s, j) for s in range(FFN_SUBTILES) for j in range(N_FFN_CHUNKS)]
    state = {}

    def issue(n):
        s, j = units[n]
        if j == 0:
            state[s] = list(prologue(s)) + [jnp.zeros((t, D_MODEL), F32)]
        up_proj(state[s][2], j, n % HU_SLOTS)

    for n in range(FFN_LOOKAHEAD):
        issue(n)
    for n, (s, j) in enumerate(units):
        if n + FFN_LOOKAHEAD < len(units):
            issue(n + FFN_LOOKAHEAD)
        gate = conv(n % HU_SLOTS, 0, j * FFN_CHUNK)
        up = conv(n % HU_SLOTS, 1, D_FF + j * FFN_CHUNK)
        act = (jax.nn.silu(gate) * up).astype(BF16)
        state[s][3] = state[s][3] + jnp.dot(act, wdown_ref[j * FFN_CHUNK:(j + 1) * FFN_CHUNK, :],
                                            preferred_element_type=F32)
        if j == N_FFN_CHUNKS - 1:
            x1, e, _, acc = state.pop(s)
            epilogue(s, x1, e, acc)


def _ffn(x, a, f, p, w_out, ffn_gain, w_up, conv_w, conv_b, w_down, w_ple, ple_gain, w_gate, b_gate):
    nb, s, _ = x.shape
    t = FFN_SUBTILES * FFN_TILE
    r = t // FFN_HALO
    last = s // FFN_HALO - 1

    def specs(w):
        return [pl.BlockSpec((1, t, w), lambda b, i: (b, i, 0)),
                pl.BlockSpec((1, FFN_HALO, w), lambda b, i: (b, jnp.maximum(i * r - 1, 0), 0)),
                pl.BlockSpec((1, FFN_HALO, w), lambda b, i: (b, jnp.minimum((i + 1) * r, last), 0))]

    f_specs = [pl.BlockSpec((1, F_SLABS, t, LANES), lambda b, i: (b, 0, i, 0)),
               pl.BlockSpec((1, F_SLABS, FFN_HALO, LANES), lambda b, i: (b, 0, jnp.maximum(i * r - 1, 0), 0)),
               pl.BlockSpec((1, F_SLABS, FFN_HALO, LANES), lambda b, i: (b, 0, jnp.minimum((i + 1) * r, last), 0))]

    in_specs = (specs(D_MODEL) + specs(ATTN_WIDTH) + f_specs
                + [pl.BlockSpec((1, t, PLE_DIM), lambda b, i: (b, i, 0)),
                   _const_spec((D_MODEL, D_MODEL)), _const_spec((1, D_MODEL)),
                   _const_spec((D_MODEL, 2 * D_FF)), _const_spec((3, 2 * D_FF)), _const_spec((1, 2 * D_FF)),
                   _const_spec((D_FF, D_MODEL)), _const_spec((PLE_DIM, D_MODEL)), _const_spec((1, D_MODEL)),
                   _const_spec((D_MODEL, D_MODEL)), _const_spec((1, D_MODEL))])
    return pl.pallas_call(
        _ffn_kernel,
        grid=(nb, s // t),
        in_specs=in_specs,
        out_specs=pl.BlockSpec((1, t, D_MODEL), lambda b, i: (b, i, 0)),
        out_shape=jax.ShapeDtypeStruct((nb, s, D_MODEL), F32),
        scratch_shapes=[pltpu.VMEM((HU_SLOTS, 2, FFN_CHUNK // LANES, FFN_TILE + 2 * UP_HALO, LANES), F32),
                        pltpu.VMEM((D_MODEL // LANES, FFN_TILE, LANES), F32)],
        compiler_params=_COMPILER_PARAMS,
        name="ffn",
    )(x, x, x, a, a, a, f, f, f, p, w_out, ffn_gain, w_up, conv_w, conv_b, w_down, w_ple, ple_gain,
      w_gate, b_gate)


def _dft_tables():
    n = np.arange(DFT_N)
    ang = 2.0 * np.pi * ((n[:, None] * n[None, :]) % DFT_N) / DFT_N
    cn, sn = np.cos(ang), np.sin(ang)
    m1 = np.block([[cn, -sn], [-sn, -cn]])
    seq = DFT_N * DFT_N
    tw = 2.0 * np.pi * ((n[:, None] * n[None, :]) % seq) / seq
    ortho = 1.0 / np.sqrt(seq * FOURIER_GROUP_DIM)
    c = np.arange(FOURIER_GROUP_DIM)
    ang_c = 2.0 * np.pi * ((c[:, None] * c[None, :]) % FOURIER_GROUP_DIM) / FOURIER_GROUP_DIM
    eye = np.eye(MXU_DIM // FOURIER_GROUP_DIM)
    cs = np.concatenate([np.kron(eye, np.cos(ang_c)), np.kron(eye, np.sin(ang_c))], axis=1)
    hsum = np.kron(np.eye(MXU_DIM // HEAD_DIM), np.ones((HEAD_DIM, HEAD_DIM)))
    f32 = lambda a: jnp.asarray(a, dtype=F32)
    return (f32(m1), f32(cn * ortho), f32(sn * ortho), f32(np.cos(tw)), f32(np.sin(tw)), f32(cs), f32(hsum))


def _rope_tables(seq):
    inv_freq = ROPE_THETA ** (-jnp.arange(0, ROT_DIM, 2, dtype=F32) / ROT_DIM)
    inv_head = jnp.concatenate([inv_freq, inv_freq, jnp.zeros((HEAD_DIM - ROT_DIM,), F32)])
    inv_lane = jnp.concatenate([inv_head] * (LANES // HEAD_DIM))
    ang = jnp.arange(seq, dtype=F32)[:, None] * inv_lane[None, :]
    return jnp.cos(ang), jnp.sin(ang)


_HEAD_ORDER = tuple(g * GQA_GROUP + i for i in range(GQA_GROUP) for g in range(N_KV_HEADS))


def _permute_heads(a, axis):
    blocks = [lax.slice_in_dim(a, h * HEAD_DIM, (h + 1) * HEAD_DIM, axis=axis) for h in _HEAD_ORDER]
    return jnp.concatenate(blocks, axis=axis)


def _block_diag(w):
    g, c, e = w.shape
    eye = jnp.eye(g, dtype=w.dtype)
    return (eye[:, None, :, None] * w[:, :, None, :]).reshape(g * c, g * e)


def _layer(xs, ps, prm, tables):
    (attn_norm, w_in, q_norm, k_norm, sink, w_fourier, attn_out_norm, fourier_out_norm, w_out, ffn_norm,
     w_up, conv_w, conv_b, w_down, w_ple, ple_norm, w_ple_gate, b_ple_gate) = prm
    m1, cn, sn, wc, ws, cs_bd, hsum = tables
    row = lambda v: v.reshape(1, -1)
    two = lambda v: jnp.concatenate([v] * (LANES // HEAD_DIM)).reshape(1, LANES)

    w_in_p = jnp.concatenate([_permute_heads(w_in[:, :ATTN_WIDTH], 1), w_in[:, ATTN_WIDTH:]], axis=1)
    w_out_p = jnp.concatenate([_permute_heads(w_out[:ATTN_WIDTH], 0), w_out[ATTN_WIDTH:]], axis=0)
    proj = []
    for x in xs:
        assert x.shape[1] == DFT_N * DFT_N
        cos_t, sin_t = _rope_tables(x.shape[1])
        proj.append(_inproj(x, row(attn_norm), w_in_p.astype(BF16), two(q_norm), two(k_norm), cos_t, sin_t,
                            hsum.astype(BF16)))
    ys = [_dft_stage1(m1.astype(BF16), cs_bd.astype(BF16), u) for _, _, _, u in proj]
    attns = [_attention(sink, q, k, v, row(_permute_heads(attn_out_norm, 0))) for q, k, v, _ in proj]
    fours = [_dft_stage2(cn, sn, wc, ws, y, _block_diag(w_fourier).astype(BF16), row(fourier_out_norm))
             for y in ys]
    return [_ffn(x, attn, four, p, w_out_p.astype(BF16), row(ffn_norm), w_up.astype(BF16), conv_w,
                 row(conv_b), w_down.astype(BF16), w_ple.astype(BF16), row(ple_norm),
                 w_ple_gate.astype(BF16), row(b_ple_gate))
            for x, attn, four, p in zip(xs, attns, fours, ps)]


def kernel(x_prompt, x_sample, p_prompt, p_sample, attn_norm, w_in, q_norm, k_norm, sink, w_fourier,
           attn_out_norm, fourier_out_norm, w_out, ffn_norm, w_up, conv_w, conv_b, w_down, w_ple, ple_norm,
           w_ple_gate, b_ple_gate):
    stacked = (attn_norm, w_in, q_norm, k_norm, sink, w_fourier, attn_out_norm, fourier_out_norm, w_out,
               ffn_norm, w_up, conv_w, conv_b, w_down, w_ple, ple_norm, w_ple_gate, b_ple_gate)
    tables = _dft_tables()
    y_prompt, y_sample = x_prompt, x_sample
    for i in range(attn_norm.shape[0]):
        prm = tuple(w[i] for w in stacked)
        y_prompt, y_sample = _layer([y_prompt, y_sample], [p_prompt[i], p_sample[i]], prm, tables)
    return (y_prompt, y_sample)
```

```python
import functools

import numpy as np
import jax
import jax.numpy as jnp
from jax import lax
from jax.experimental import pallas as pl
from jax.experimental.pallas import tpu as pltpu

F32 = jnp.float32
BF16 = jnp.bfloat16

D_MODEL = 1024
HEAD_DIM = 64
N_HEADS = 8
N_KV_HEADS = 2
GQA_GROUP = N_HEADS // N_KV_HEADS
ATTN_WIDTH = N_HEADS * HEAD_DIM
KV_WIDTH = N_KV_HEADS * HEAD_DIM
FOURIER_WIDTH = 512
FOURIER_GROUP_DIM = 64
N_FOURIER_GROUPS = FOURIER_WIDTH // FOURIER_GROUP_DIM
IN_WIDTH = ATTN_WIDTH + 2 * KV_WIDTH + FOURIER_WIDTH
BLOCK = 128
ROPE_THETA = 500000.0
ROT_DIM = HEAD_DIM // 4
ROT_HALF = ROT_DIM // 2
D_FF = 2816
PLE_DIM = 256
EPS = 1e-6
LOG2E = 1.4426950408889634

LANES = 128
MXU_DIM = 256
BF16_ROWS = 16
VMEM_LIMIT_BYTES = 56 * 1024 * 1024

DFT_N = 128

IN_TILE = 1024
IN_SUB = 128
ATTN_TILE = 1024
ATTN_LOOKAHEAD = 2
FFN_LOOKAHEAD = 2
F32_ROWS = 8
S1_GROUP = F32_ROWS
F_SLABS = FOURIER_WIDTH // LANES
S2_K1 = F32_ROWS
FFN_TILE = 512
FFN_SUBTILES = 1
FFN_HALO = BF16_ROWS
UP_HALO = F32_ROWS
FFN_CHUNK = 256
N_FFN_CHUNKS = D_FF // FFN_CHUNK
HU_SLOTS = FFN_LOOKAHEAD + 1


_COMPILER_PARAMS = pltpu.CompilerParams(
    dimension_semantics=("parallel", "parallel"),
    vmem_limit_bytes=VMEM_LIMIT_BYTES,
)


def _rms(x, gain, n):
    ms = jnp.sum(x * x, axis=-1, keepdims=True) * (1.0 / n)
    return x * lax.rsqrt(ms + EPS) * gain


def _const_spec(shape):
    zeros = (0,) * len(shape)
    return pl.BlockSpec(shape, lambda *_: zeros, pipeline_mode=pl.Buffered(1))


def _inproj_kernel(x_ref, g_ref, win_ref, qg_ref, kg_ref, cosr_ref, sinr_ref, coss_ref, sins_ref, hsum_ref,
                   q_ref, k_ref, v_ref, u_ref):
    step = pl.program_id(1)
    cos_a = coss_ref[pl.ds(step, 1), :]
    sin_a = sins_ref[pl.ds(step, 1), :]
    d = lax.broadcasted_iota(jnp.int32, (IN_SUB, LANES), 1) % HEAD_DIM
    scale = HEAD_DIM ** -0.5 * LOG2E

    def project(r0):
        h = _rms(x_ref[0, r0:r0 + IN_SUB], g_ref[...], D_MODEL).astype(BF16)
        return jnp.dot(h, win_ref[...], preferred_element_type=F32)

    def finish(r0, z):
        cos_b = cosr_ref[r0:r0 + IN_SUB]
        sin_b = sinr_ref[r0:r0 + IN_SUB]
        cos_t = cos_a * cos_b - sin_a * sin_b
        sin_t = sin_a * cos_b + cos_a * sin_b
        sin_lo = jnp.where(d < ROT_HALF, -sin_t, 0.0)
        sin_hi = jnp.where(d >= ROT_HALF, sin_t, 0.0)

        def norm_rope(t, gain):
            w = t.shape[1]
            wide = lambda a: jnp.concatenate([a] * (w // LANES), axis=1)
            ssq = jnp.dot((t * t).astype(BF16), hsum_ref[:w, :w], preferred_element_type=F32)
            tn = t * lax.rsqrt(ssq * (1.0 / HEAD_DIM) + EPS) * wide(gain)
            up = pltpu.roll(tn, w - ROT_HALF, 1)
            dn = pltpu.roll(tn, ROT_HALF, 1)
            return tn * wide(cos_t) + up * wide(sin_lo) + dn * wide(sin_hi)

        for t in range(ATTN_WIDTH // MXU_DIM):
            qt = norm_rope(z[:, t * MXU_DIM:(t + 1) * MXU_DIM], qg_ref[...])
            q_ref[0, r0:r0 + IN_SUB, t * MXU_DIM:(t + 1) * MXU_DIM] = (qt * scale).astype(BF16)
        k_ref[0, r0:r0 + IN_SUB] = norm_rope(z[:, ATTN_WIDTH:ATTN_WIDTH + KV_WIDTH], kg_ref[...]).astype(BF16)
        v_ref[0, r0:r0 + IN_SUB] = z[:, ATTN_WIDTH + KV_WIDTH:ATTN_WIDTH + 2 * KV_WIDTH].astype(BF16)
        for t in range(F_SLABS):
            lo = ATTN_WIDTH + 2 * KV_WIDTH + t * LANES
            u_ref[0, t, r0:r0 + IN_SUB] = z[:, lo:lo + LANES]

    starts = list(range(0, IN_TILE, IN_SUB))
    z = project(starts[0])
    for n, r0 in enumerate(starts):
        z_next = project(starts[n + 1]) if n + 1 < len(starts) else None
        finish(r0, z)
        z = z_next


def _inproj(x, attn_norm, w_in, q_gain, k_gain, rope, hsum):
    nb, s, _ = x.shape
    t = IN_TILE
    tok = lambda w: pl.BlockSpec((1, t, w), lambda b, i: (b, i, 0))
    within = _const_spec((t, LANES))
    per_step = _const_spec((s // t, LANES))
    out_shape = (
        jax.ShapeDtypeStruct((nb, s, ATTN_WIDTH), BF16),
        jax.ShapeDtypeStruct((nb, s, KV_WIDTH), BF16),
        jax.ShapeDtypeStruct((nb, s, KV_WIDTH), BF16),
        jax.ShapeDtypeStruct((nb, F_SLABS, s, LANES), F32),
    )
    return pl.pallas_call(
        _inproj_kernel,
        grid=(nb, s // t),
        in_specs=[tok(D_MODEL), _const_spec((1, D_MODEL)), _const_spec((D_MODEL, IN_WIDTH)),
                  _const_spec((1, LANES)), _const_spec((1, LANES)), within, within, per_step, per_step,
                  _const_spec((MXU_DIM, MXU_DIM))],
        out_specs=(tok(ATTN_WIDTH), tok(KV_WIDTH), tok(KV_WIDTH),
                   pl.BlockSpec((1, F_SLABS, t, LANES), lambda b, i: (b, 0, i, 0))),
        out_shape=out_shape,
        compiler_params=_COMPILER_PARAMS,
        name="inproj",
    )(x, attn_norm, w_in, q_gain, k_gain, *rope, hsum)


def _attn_kernel(sink_ref, q_ref, kp_ref, km_ref, kn_ref, vp_ref, vm_ref, vn_ref, g_ref, o_ref, *, seq):
    i = pl.program_id(1)
    kcat = jnp.concatenate([kp_ref[0], km_ref[0], kn_ref[0]], axis=0)
    vcat = jnp.concatenate([vp_ref[0], vm_ref[0], vn_ref[0]], axis=0)
    nkeys = 3 * BLOCK
    diff = (lax.broadcasted_iota(jnp.int32, (BLOCK, BLOCK), 1)
            - lax.broadcasted_iota(jnp.int32, (BLOCK, BLOCK), 0))
    lane = lax.broadcasted_iota(jnp.int32, (BLOCK, LANES), 1)
    low = lane < HEAD_DIM
    gain = g_ref[...]
    units = [(j, g) for j in range(ATTN_TILE // BLOCK) for g in range(N_KV_HEADS)]

    def scores(j, g):
        kw = kcat[j * BLOCK:j * BLOCK + nkeys]
        sel = low if g == 0 else jnp.logical_not(low)
        qs = []
        for hh in range(GQA_GROUP):
            qt = q_ref[0, j * BLOCK:(j + 1) * BLOCK, hh * LANES:(hh + 1) * LANES]
            qs.append(jnp.where(sel, qt, jnp.zeros_like(qt)))
        qg = jnp.concatenate(qs, axis=0)
        return lax.dot_general(qg, kw, (((1,), (1,)), ((), ())), preferred_element_type=F32)

    def softmax_pv(j, g, s):
        qblk = i * (ATTN_TILE // BLOCK) + j
        lo = jnp.where(qblk > 0, 0, BLOCK)
        hi = jnp.where(qblk < seq // BLOCK - 1, 0, -BLOCK)
        mask_prev = diff >= lo
        mask_next = diff <= hi
        vw = vcat[j * BLOCK:j * BLOCK + nkeys]
        ps, rs = [], []
        for hh in range(GQA_GROUP):
            sk = sink_ref[g * GQA_GROUP + hh] * LOG2E
            sh = s[hh * BLOCK:(hh + 1) * BLOCK]
            sh = jnp.concatenate([jnp.where(mask_prev, sh[:, :BLOCK], -jnp.inf), sh[:, BLOCK:2 * BLOCK],
                                  jnp.where(mask_next, sh[:, 2 * BLOCK:], -jnp.inf)], axis=1)
            m = jnp.maximum(jnp.max(sh, axis=-1, keepdims=True), sk)
            p = jnp.exp2(sh - m)
            denom = jnp.sum(p, axis=-1, keepdims=True) + jnp.exp2(sk - m)
            ps.append(p.astype(BF16))
            rs.append(1.0 / denom)
        p = jnp.concatenate(ps, axis=0)
        o = jnp.dot(p, vw, preferred_element_type=F32)
        return [o[hh * BLOCK:(hh + 1) * BLOCK] * rs[hh] for hh in range(GQA_GROUP)]

    pending = [scores(*u) for u in units[:ATTN_LOOKAHEAD]]
    outs = []
    for n, (j, g) in enumerate(units):
        if n + ATTN_LOOKAHEAD < len(units):
            pending.append(scores(*units[n + ATTN_LOOKAHEAD]))
        outs.append(softmax_pv(j, g, pending.pop(0)))
        if g == N_KV_HEADS - 1:
            a = jnp.concatenate([jnp.where(low, outs[0][hh], outs[1][hh]) for hh in range(GQA_GROUP)], axis=1)
            o_ref[0, j * BLOCK:(j + 1) * BLOCK, :] = _rms(a, gain, ATTN_WIDTH).astype(BF16)
            outs = []


def _attention(sink, q, k, v, out_gain):
    nb, s, _ = q.shape
    t = ATTN_TILE
    r = t // BLOCK
    last = s // BLOCK - 1
    main = lambda w: pl.BlockSpec((1, t, w), lambda b, i, *_: (b, i, 0))
    prev = pl.BlockSpec((1, BLOCK, KV_WIDTH), lambda b, i, *_: (b, jnp.maximum(i * r - 1, 0), 0))
    nxt = pl.BlockSpec((1, BLOCK, KV_WIDTH), lambda b, i, *_: (b, jnp.minimum((i + 1) * r, last), 0))
    grid_spec = pltpu.PrefetchScalarGridSpec(
        num_scalar_prefetch=1,
        grid=(nb, s // t),
        in_specs=[main(ATTN_WIDTH), prev, main(KV_WIDTH), nxt, prev, main(KV_WIDTH), nxt,
                  pl.BlockSpec((1, ATTN_WIDTH), lambda b, i, *_: (0, 0))],
        out_specs=main(ATTN_WIDTH),
    )
    return pl.pallas_call(
        functools.partial(_attn_kernel, seq=s),
        grid_spec=grid_spec,
        out_shape=jax.ShapeDtypeStruct((nb, s, ATTN_WIDTH), BF16),
        compiler_params=_COMPILER_PARAMS,
        name="attention",
    )(sink, q, k, k, k, v, v, v, out_gain)


def _dft1_kernel(m1_ref, cs_ref, u_ref, y_ref):
    m1 = m1_ref[...]
    cs = cs_ref[...]

    def channel_dft(j):
        slabs = []
        for t in range(F_SLABS):
            rows = u_ref.at[0, t].reshape(DFT_N * S1_GROUP, LANES)
            slabs.append(rows[pl.ds(j, DFT_N, stride=S1_GROUP), :].astype(BF16))
        per = MXU_DIM // LANES
        halves = [jnp.dot(jnp.concatenate(slabs[c * per:(c + 1) * per], axis=1), cs, preferred_element_type=F32)
                  for c in range(F_SLABS // per)]
        a = jnp.concatenate([h[:, :MXU_DIM] for h in halves], axis=1)
        b = jnp.concatenate([h[:, MXU_DIM:] for h in halves], axis=1)
        return jnp.concatenate([a, b], axis=0).astype(BF16)

    ab = [channel_dft(j) for j in range(S1_GROUP)]
    for j in range(S1_GROUP):
        y = jnp.dot(m1, ab[j], preferred_element_type=F32)
        y_ref[0, :, j * FOURIER_WIDTH:(j + 1) * FOURIER_WIDTH] = y.astype(BF16)


def _dft_stage1(m1, cs, u):
    nb = u.shape[0]
    groups = DFT_N // S1_GROUP
    u = u.reshape(nb, F_SLABS, DFT_N, groups, S1_GROUP, LANES)
    cols = S1_GROUP * FOURIER_WIDTH
    return pl.pallas_call(
        _dft1_kernel,
        grid=(nb, groups),
        in_specs=[_const_spec((2 * DFT_N, 2 * DFT_N)), _const_spec((MXU_DIM, 2 * MXU_DIM)),
                  pl.BlockSpec((1, F_SLABS, DFT_N, 1, S1_GROUP, LANES), lambda n, g: (n, 0, 0, g, 0, 0))],
        out_specs=pl.BlockSpec((1, 2 * DFT_N, cols), lambda n, g: (n, 0, g)),
        out_shape=jax.ShapeDtypeStruct((nb, 2 * DFT_N, DFT_N * FOURIER_WIDTH), BF16),
        compiler_params=_COMPILER_PARAMS,
        name="dft_stage1",
    )(m1, cs, u)


def _dft2_kernel(cn_ref, sn_ref, wc_ref, ws_ref, y_ref, wf_ref, g_ref, o_ref):
    cn = cn_ref[...]
    sn = sn_ref[...]
    wf = wf_ref[...]
    gain = g_ref[...]

    def seq_dft(j):
        wc = wc_ref[j:j + 1, :]
        ws = ws_ref[j:j + 1, :]
        gc = cn * wc - sn * ws
        gs = sn * wc + cn * ws
        g = jnp.concatenate([gc, gs], axis=1).astype(BF16)
        y = jnp.concatenate([y_ref[0, 0, j], y_ref[0, 1, j]], axis=0)
        return jnp.dot(g, y, preferred_element_type=F32).astype(BF16)

    xr = [seq_dft(j) for j in range(S2_K1)]
    fs = [jnp.dot(xr[j], wf, preferred_element_type=F32) for j in range(S2_K1)]
    for j in range(S2_K1):
        fn = _rms(fs[j], gain, FOURIER_WIDTH)
        for t in range(F_SLABS):
            rows = o_ref.at[0, t].reshape(DFT_N * S2_K1, LANES)
            rows[pl.ds(j, DFT_N, stride=S2_K1), :] = fn[:, t * LANES:(t + 1) * LANES]


def _dft_stage2(cn, sn, wc, ws, y, wf_bd, out_gain):
    nb = y.shape[0]
    y = y.reshape(nb, 2, DFT_N, DFT_N, FOURIER_WIDTH)
    groups = DFT_N // S2_K1
    tw = pl.BlockSpec((S2_K1, DFT_N), lambda n, k: (k, 0))
    out = pl.pallas_call(
        _dft2_kernel,
        grid=(nb, DFT_N // S2_K1),
        in_specs=[_const_spec((DFT_N, DFT_N)), _const_spec((DFT_N, DFT_N)), tw, tw,
                  pl.BlockSpec((1, 2, S2_K1, DFT_N, FOURIER_WIDTH), lambda n, k: (n, 0, k, 0, 0)),
                  _const_spec((FOURIER_WIDTH, FOURIER_WIDTH)), _const_spec((1, FOURIER_WIDTH))],
        out_specs=pl.BlockSpec((1, F_SLABS, DFT_N, 1, S2_K1, LANES), lambda n, k: (n, 0, 0, k, 0, 0)),
        out_shape=jax.ShapeDtypeStruct((nb, F_SLABS, DFT_N, groups, S2_K1, LANES), F32),
        compiler_params=_COMPILER_PARAMS,
        name="dft_stage2",
    )(cn, sn, wc, ws, y, wf_bd, out_gain)
    return out.reshape(nb, F_SLABS, DFT_N * DFT_N, LANES)


def _ffn_kernel(xm_ref, xp_ref, xn_ref, am_ref, ap_ref, an_ref, fm_ref, fp_ref, fn_ref, p_ref,
                wout_ref, gffn_ref, wup_ref, cw_ref, cb_ref, wdown_ref, wple_ref, gple_ref, wgate_ref,
                bgate_ref, o_ref, hu_scr, nat_scr):
    i = pl.program_id(1)
    nt = pl.num_programs(1)
    t = FFN_TILE
    half = t // 2
    ext = t + 2 * FFN_HALO
    trim = FFN_HALO - UP_HALO
    up_ext = t + 2 * UP_HALO
    x_all = jnp.concatenate([xp_ref[0], xm_ref[0], xn_ref[0]], axis=0)
    a_all = jnp.concatenate([ap_ref[0], am_ref[0], an_ref[0]], axis=0)
    f_all = [jnp.concatenate([fp_ref[0, s], fm_ref[0, s], fn_ref[0, s]], axis=0).astype(BF16)
             for s in range(F_SLABS)]
    mixed_all = jnp.concatenate([a_all] + f_all, axis=1)
    row = lax.broadcasted_iota(jnp.int32, (ext, 1), 0)

    def prologue(s):
        r0 = s * t
        x1 = x_all[r0:r0 + ext] + jnp.dot(mixed_all[r0:r0 + ext], wout_ref[...], preferred_element_type=F32)
        e = jnp.dot(p_ref[0, r0:r0 + t].astype(BF16), wple_ref[...], preferred_element_type=F32)
        e = _rms(e, gple_ref[...], D_MODEL)
        h = _rms(x1, gffn_ref[...], D_MODEL)
        first_valid = jnp.where(i == 0, FFN_HALO, 0) if s == 0 else 0
        end_valid = jnp.where(i == nt - 1, t + FFN_HALO, ext) if s == FFN_SUBTILES - 1 else ext
        h = jnp.where((row >= first_valid) & (row < end_valid), h, 0.0)
        return x1[FFN_HALO:FFN_HALO + t], e, h[trim:trim + up_ext].astype(BF16)

    def up_proj(h, j, slot):
        for part, c0 in enumerate((j * FFN_CHUNK, D_FF + j * FFN_CHUNK)):
            hu = jnp.dot(h, wup_ref[:, c0:c0 + FFN_CHUNK], preferred_element_type=F32)
            for c in range(FFN_CHUNK // LANES):
                hu_scr[slot, part, c] = hu[:, c * LANES:(c + 1) * LANES]

    def conv(slot, part, c0):
        cols = []
        for c in range(FFN_CHUNK // LANES):
            cw = cw_ref[:, c0 + c * LANES:c0 + (c + 1) * LANES]
            cb = cb_ref[:, c0 + c * LANES:c0 + (c + 1) * LANES]
            rows = [hu_scr[slot, part, c, pl.ds(UP_HALO - 1 + k, half, stride=2), :] for k in range(4)]
            even = rows[0] * cw[0:1] + rows[1] * cw[1:2] + rows[2] * cw[2:3] + cb
            odd = rows[1] * cw[0:1] + rows[2] * cw[1:2] + rows[3] * cw[2:3] + cb
            cols.append(jnp.concatenate([even, odd], axis=0))
        return jnp.concatenate(cols, axis=1)

    def epilogue(s, x1, e, acc):
        for c in range(D_MODEL // LANES):
            nat_scr[c, pl.ds(0, half, stride=2), :] = acc[:half, c * LANES:(c + 1) * LANES]
            nat_scr[c, pl.ds(1, half, stride=2), :] = acc[half:, c * LANES:(c + 1) * LANES]
        x2 = x1 + jnp.concatenate([nat_scr[c] for c in range(D_MODEL // LANES)], axis=1)
        gate = jax.nn.sigmoid(jnp.dot(x2.astype(BF16), wgate_ref[...], preferred_element_type=F32)
                              + bgate_ref[...])
        o_ref[0, s * t:(s + 1) * t, :] = x2 + gate * e

    units = [(s, j) for s in range(FFN_SUBTILES) for j in range(N_FFN_CHUNKS)]
    state = {}

    def issue(n):
        s, j = units[n]
        if j == 0:
            state[s] = list(prologue(s)) + [jnp.zeros((t, D_MODEL), F32)]
        up_proj(state[s][2], j, n % HU_SLOTS)

    for n in range(FFN_LOOKAHEAD):
        issue(n)
    for n, (s, j) in enumerate(units):
        if n + FFN_LOOKAHEAD < len(units):
            issue(n + FFN_LOOKAHEAD)
        gate = conv(n % HU_SLOTS, 0, j * FFN_CHUNK)
        up = conv(n % HU_SLOTS, 1, D_FF + j * FFN_CHUNK)
        act = (jax.nn.silu(gate) * up).astype(BF16)
        state[s][3] = state[s][3] + jnp.dot(act, wdown_ref[j * FFN_CHUNK:(j + 1) * FFN_CHUNK, :],
                                            preferred_element_type=F32)
        if j == N_FFN_CHUNKS - 1:
            x1, e, _, acc = state.pop(s)
            epilogue(s, x1, e, acc)


def _ffn(x, a, f, p, w_out, ffn_gain, w_up, conv_w, conv_b, w_down, w_ple, ple_gain, w_gate, b_gate):
    nb, s, _ = x.shape
    t = FFN_SUBTILES * FFN_TILE
    r = t // FFN_HALO
    last = s // FFN_HALO - 1

    def specs(w):
        return [pl.BlockSpec((1, t, w), lambda b, i: (b, i, 0)),
                pl.BlockSpec((1, FFN_HALO, w), lambda b, i: (b, jnp.maximum(i * r - 1, 0), 0)),
                pl.BlockSpec((1, FFN_HALO, w), lambda b, i: (b, jnp.minimum((i + 1) * r, last), 0))]

    f_specs = [pl.BlockSpec((1, F_SLABS, t, LANES), lambda b, i: (b, 0, i, 0)),
               pl.BlockSpec((1, F_SLABS, FFN_HALO, LANES), lambda b, i: (b, 0, jnp.maximum(i * r - 1, 0), 0)),
               pl.BlockSpec((1, F_SLABS, FFN_HALO, LANES), lambda b, i: (b, 0, jnp.minimum((i + 1) * r, last), 0))]

    in_specs = (specs(D_MODEL) + specs(ATTN_WIDTH) + f_specs
                + [pl.BlockSpec((1, t, PLE_DIM), lambda b, i: (b, i, 0)),
                   _const_spec((D_MODEL, D_MODEL)), _const_spec((1, D_MODEL)),
                   _const_spec((D_MODEL, 2 * D_FF)), _const_spec((3, 2 * D_FF)), _const_spec((1, 2 * D_FF)),
                   _const_spec((D_FF, D_MODEL)), _const_spec((PLE_DIM, D_MODEL)), _const_spec((1, D_MODEL)),
                   _const_spec((D_MODEL, D_MODEL)), _const_spec((1, D_MODEL))])
    return pl.pallas_call(
        _ffn_kernel,
        grid=(nb, s // t),
        in_specs=in_specs,
        out_specs=pl.BlockSpec((1, t, D_MODEL), lambda b, i: (b, i, 0)),
        out_shape=jax.ShapeDtypeStruct((nb, s, D_MODEL), F32),
        scratch_shapes=[pltpu.VMEM((HU_SLOTS, 2, FFN_CHUNK // LANES, FFN_TILE + 2 * UP_HALO, LANES), F32),
                        pltpu.VMEM((D_MODEL // LANES, FFN_TILE, LANES), F32)],
        compiler_params=_COMPILER_PARAMS,
        name="ffn",
    )(x, x, x, a, a, a, f, f, f, p, w_out, ffn_gain, w_up, conv_w, conv_b, w_down, w_ple, ple_gain,
      w_gate, b_gate)


def _dft_tables():
    n = np.arange(DFT_N)
    ang = 2.0 * np.pi * ((n[:, None] * n[None, :]) % DFT_N) / DFT_N
    cn, sn = np.cos(ang), np.sin(ang)
    m1 = np.block([[cn, -sn], [-sn, -cn]])
    seq = DFT_N * DFT_N
    tw = 2.0 * np.pi * ((n[:, None] * n[None, :]) % seq) / seq
    ortho = 1.0 / np.sqrt(seq * FOURIER_GROUP_DIM)
    c = np.arange(FOURIER_GROUP_DIM)
    ang_c = 2.0 * np.pi * ((c[:, None] * c[None, :]) % FOURIER_GROUP_DIM) / FOURIER_GROUP_DIM
    eye = np.eye(MXU_DIM // FOURIER_GROUP_DIM)
    cs = np.concatenate([np.kron(eye, np.cos(ang_c)), np.kron(eye, np.sin(ang_c))], axis=1)
    hsum = np.kron(np.eye(MXU_DIM // HEAD_DIM), np.ones((HEAD_DIM, HEAD_DIM)))
    f32 = lambda a: jnp.asarray(a, dtype=F32)
    return (f32(m1), f32(cn * ortho), f32(sn * ortho), f32(np.cos(tw)), f32(np.sin(tw)), f32(cs), f32(hsum))


def _rope_tables(seq):
    inv_freq = ROPE_THETA ** (-jnp.arange(0, ROT_DIM, 2, dtype=F32) / ROT_DIM)
    inv_head = jnp.concatenate([inv_freq, inv_freq, jnp.zeros((HEAD_DIM - ROT_DIM,), F32)])
    inv_lane = jnp.concatenate([inv_head] * (LANES // HEAD_DIM))[None, :]
    within = jnp.arange(IN_TILE, dtype=F32)[:, None] * inv_lane
    per_step = (jnp.arange(seq // IN_TILE, dtype=F32) * IN_TILE)[:, None] * inv_lane
    return jnp.cos(within), jnp.sin(within), jnp.cos(per_step), jnp.sin(per_step)


_HEAD_ORDER = tuple(g * GQA_GROUP + i for i in range(GQA_GROUP) for g in range(N_KV_HEADS))


def _permute_heads(a, axis):
    blocks = [lax.slice_in_dim(a, h * HEAD_DIM, (h + 1) * HEAD_DIM, axis=axis) for h in _HEAD_ORDER]
    return jnp.concatenate(blocks, axis=axis)


def _block_diag(w):
    g, c, e = w.shape
    eye = jnp.eye(g, dtype=w.dtype)
    return (eye[:, None, :, None] * w[:, :, None, :]).reshape(g * c, g * e)


def _layer(xs, ps, prm, tables):
    (attn_norm, w_in, q_norm, k_norm, sink, w_fourier, attn_out_norm, fourier_out_norm, w_out, ffn_norm,
     w_up, conv_w, conv_b, w_down, w_ple, ple_norm, w_ple_gate, b_ple_gate) = prm
    m1, cn, sn, wc, ws, cs_bd, hsum = tables
    row = lambda v: v.reshape(1, -1)
    two = lambda v: jnp.concatenate([v] * (LANES // HEAD_DIM)).reshape(1, LANES)

    w_in_p = jnp.concatenate([_permute_heads(w_in[:, :ATTN_WIDTH], 1), w_in[:, ATTN_WIDTH:]], axis=1)
    w_out_p = jnp.concatenate([_permute_heads(w_out[:ATTN_WIDTH], 0), w_out[ATTN_WIDTH:]], axis=0)
    proj = []
    for x in xs:
        assert x.shape[1] == DFT_N * DFT_N
        proj.append(_inproj(x, row(attn_norm), w_in_p.astype(BF16), two(q_norm), two(k_norm),
                            _rope_tables(x.shape[1]),
                            hsum.astype(BF16)))
    ys = [_dft_stage1(m1.astype(BF16), cs_bd.astype(BF16), u) for _, _, _, u in proj]
    attns = [_attention(sink, q, k, v, row(_permute_heads(attn_out_norm, 0))) for q, k, v, _ in proj]
    fours = [_dft_stage2(cn, sn, wc, ws, y, _block_diag(w_fourier).astype(BF16), row(fourier_out_norm))
             for y in ys]
    return [_ffn(x, attn, four, p, w_out_p.astype(BF16), row(ffn_norm), w_up.astype(BF16), conv_w,
                 row(conv_b), w_down.astype(BF16), w_ple.astype(BF16), row(ple_norm),
                 w_ple_gate.astype(BF16), row(b_ple_gate))
            for x, attn, four, p in zip(xs, attns, fours, ps)]


def kernel(x_prompt, x_sample, p_prompt, p_sample, attn_norm, w_in, q_norm, k_norm, sink, w_fourier,
           attn_out_norm, fourier_out_norm, w_out, ffn_norm, w_up, conv_w, conv_b, w_down, w_ple, ple_norm,
           w_ple_gate, b_ple_gate):
    stacked = (attn_norm, w_in, q_norm, k_norm, sink, w_fourier, attn_out_norm, fourier_out_norm, w_out,
               ffn_norm, w_up, conv_w, conv_b, w_down, w_ple, ple_norm, w_ple_gate, b_ple_gate)
    tables = _dft_tables()
    y_prompt, y_sample = x_prompt, x_sample
    for i in range(attn_norm.shape[0]):
        prm = tuple(w[i] for w in stacked)
        y_prompt, y_sample = _layer([y_prompt, y_sample], [p_prompt[i], p_sample[i]], prm, tables)
    return (y_prompt, y_sample)
```

```python
import functools

import numpy as np
import jax
import jax.numpy as jnp
from jax import lax
from jax.experimental import pallas as pl
from jax.experimental.pallas import tpu as pltpu

F32 = jnp.float32
BF16 = jnp.bfloat16

D_MODEL = 1024
HEAD_DIM = 64
N_HEADS = 8
N_KV_HEADS = 2
GQA_GROUP = N_HEADS // N_KV_HEADS
ATTN_WIDTH = N_HEADS * HEAD_DIM
KV_WIDTH = N_KV_HEADS * HEAD_DIM
FOURIER_WIDTH = 512
FOURIER_GROUP_DIM = 64
N_FOURIER_GROUPS = FOURIER_WIDTH // FOURIER_GROUP_DIM
IN_WIDTH = ATTN_WIDTH + 2 * KV_WIDTH + FOURIER_WIDTH
BLOCK = 128
ROPE_THETA = 500000.0
ROT_DIM = HEAD_DIM // 4
ROT_HALF = ROT_DIM // 2
D_FF = 2816
PLE_DIM = 256
EPS = 1e-6
LOG2E = 1.4426950408889634

LANES = 128
MXU_DIM = 256
BF16_ROWS = 16
VMEM_LIMIT_BYTES = 56 * 1024 * 1024

DFT_N = 128

IN_TILE = 2048
IN_SUB = 128
ATTN_TILE = 1024
ATTN_LOOKAHEAD = 2
FFN_LOOKAHEAD = 3
F32_ROWS = 8
S1_GROUP = F32_ROWS
F_SLABS = FOURIER_WIDTH // LANES
S2_K1 = F32_ROWS
FFN_TILE = 512
FFN_SUBTILES = 1
FFN_HALO = BF16_ROWS
UP_HALO = F32_ROWS
FFN_CHUNK = 256
N_FFN_CHUNKS = D_FF // FFN_CHUNK
HU_SLOTS = FFN_LOOKAHEAD + 1


_COMPILER_PARAMS = pltpu.CompilerParams(
    dimension_semantics=("parallel", "parallel"),
    vmem_limit_bytes=VMEM_LIMIT_BYTES,
)


def _rms(x, gain, n):
    ms = jnp.sum(x * x, axis=-1, keepdims=True) * (1.0 / n)
    return x * lax.rsqrt(ms + EPS) * gain


def _const_spec(shape):
    zeros = (0,) * len(shape)
    return pl.BlockSpec(shape, lambda *_: zeros, pipeline_mode=pl.Buffered(1))


def _inproj_kernel(x_ref, g_ref, win_ref, qg_ref, kg_ref, cosr_ref, sinr_ref, coss_ref, sins_ref, hsum_ref,
                   q_ref, k_ref, v_ref, u_ref):
    step = pl.program_id(1)
    cos_a = coss_ref[pl.ds(step, 1), :]
    sin_a = sins_ref[pl.ds(step, 1), :]
    d = lax.broadcasted_iota(jnp.int32, (IN_SUB, LANES), 1) % HEAD_DIM
    scale = HEAD_DIM ** -0.5 * LOG2E

    def project(r0):
        h = _rms(x_ref[0, r0:r0 + IN_SUB], g_ref[...], D_MODEL).astype(BF16)
        return jnp.dot(h, win_ref[...], preferred_element_type=F32)

    def finish(r0, z):
        cos_b = cosr_ref[r0:r0 + IN_SUB]
        sin_b = sinr_ref[r0:r0 + IN_SUB]
        cos_t = cos_a * cos_b - sin_a * sin_b
        sin_t = sin_a * cos_b + cos_a * sin_b
        sin_lo = jnp.where(d < ROT_HALF, -sin_t, 0.0)
        sin_hi = jnp.where(d >= ROT_HALF, sin_t, 0.0)

        def norm_rope(t, gain):
            w = t.shape[1]
            wide = lambda a: jnp.concatenate([a] * (w // LANES), axis=1)
            ssq = jnp.dot((t * t).astype(BF16), hsum_ref[:w, :w], preferred_element_type=F32)
            tn = t * lax.rsqrt(ssq * (1.0 / HEAD_DIM) + EPS) * wide(gain)
            up = pltpu.roll(tn, w - ROT_HALF, 1)
            dn = pltpu.roll(tn, ROT_HALF, 1)
            return tn * wide(cos_t) + up * wide(sin_lo) + dn * wide(sin_hi)

        for t in range(ATTN_WIDTH // MXU_DIM):
            qt = norm_rope(z[:, t * MXU_DIM:(t + 1) * MXU_DIM], qg_ref[...])
            q_ref[0, r0:r0 + IN_SUB, t * MXU_DIM:(t + 1) * MXU_DIM] = (qt * scale).astype(BF16)
        k_ref[0, r0:r0 + IN_SUB] = norm_rope(z[:, ATTN_WIDTH:ATTN_WIDTH + KV_WIDTH], kg_ref[...]).astype(BF16)
        v_ref[0, r0:r0 + IN_SUB] = z[:, ATTN_WIDTH + KV_WIDTH:ATTN_WIDTH + 2 * KV_WIDTH].astype(BF16)
        for t in range(F_SLABS):
            lo = ATTN_WIDTH + 2 * KV_WIDTH + t * LANES
            u_ref[0, t, r0:r0 + IN_SUB] = z[:, lo:lo + LANES]

    starts = list(range(0, IN_TILE, IN_SUB))
    z = project(starts[0])
    for n, r0 in enumerate(starts):
        z_next = project(starts[n + 1]) if n + 1 < len(starts) else None
        finish(r0, z)
        z = z_next


def _inproj(x, attn_norm, w_in, q_gain, k_gain, rope, hsum):
    nb, s, _ = x.shape
    t = IN_TILE
    tok = lambda w: pl.BlockSpec((1, t, w), lambda b, i: (b, i, 0))
    within = _const_spec((t, LANES))
    per_step = _const_spec((s // t, LANES))
    out_shape = (
        jax.ShapeDtypeStruct((nb, s, ATTN_WIDTH), BF16),
        jax.ShapeDtypeStruct((nb, s, KV_WIDTH), BF16),
        jax.ShapeDtypeStruct((nb, s, KV_WIDTH), BF16),
        jax.ShapeDtypeStruct((nb, F_SLABS, s, LANES), F32),
    )
    return pl.pallas_call(
        _inproj_kernel,
        grid=(nb, s // t),
        in_specs=[tok(D_MODEL), _const_spec((1, D_MODEL)), _const_spec((D_MODEL, IN_WIDTH)),
                  _const_spec((1, LANES)), _const_spec((1, LANES)), within, within, per_step, per_step,
                  _const_spec((MXU_DIM, MXU_DIM))],
        out_specs=(tok(ATTN_WIDTH), tok(KV_WIDTH), tok(KV_WIDTH),
                   pl.BlockSpec((1, F_SLABS, t, LANES), lambda b, i: (b, 0, i, 0))),
        out_shape=out_shape,
        compiler_params=_COMPILER_PARAMS,
        name="inproj",
    )(x, attn_norm, w_in, q_gain, k_gain, *rope, hsum)


def _attn_kernel(sink_ref, q_ref, kp_ref, km_ref, kn_ref, vp_ref, vm_ref, vn_ref, g_ref, o_ref, *, seq):
    i = pl.program_id(1)
    kcat = jnp.concatenate([kp_ref[0], km_ref[0], kn_ref[0]], axis=0)
    vcat = jnp.concatenate([vp_ref[0], vm_ref[0], vn_ref[0]], axis=0)
    nkeys = 3 * BLOCK
    diff = (lax.broadcasted_iota(jnp.int32, (BLOCK, BLOCK), 1)
            - lax.broadcasted_iota(jnp.int32, (BLOCK, BLOCK), 0))
    lane = lax.broadcasted_iota(jnp.int32, (BLOCK, LANES), 1)
    low = lane < HEAD_DIM
    gain = g_ref[...]
    units = [(j, g) for j in range(ATTN_TILE // BLOCK) for g in range(N_KV_HEADS)]

    def scores(j, g):
        kw = kcat[j * BLOCK:j * BLOCK + nkeys]
        sel = low if g == 0 else jnp.logical_not(low)
        qs = []
        for hh in range(GQA_GROUP):
            qt = q_ref[0, j * BLOCK:(j + 1) * BLOCK, hh * LANES:(hh + 1) * LANES]
            qs.append(jnp.where(sel, qt, jnp.zeros_like(qt)))
        qg = jnp.concatenate(qs, axis=0)
        return lax.dot_general(qg, kw, (((1,), (1,)), ((), ())), preferred_element_type=F32)

    def softmax_pv(j, g, s):
        qblk = i * (ATTN_TILE // BLOCK) + j
        lo = jnp.where(qblk > 0, 0, BLOCK)
        hi = jnp.where(qblk < seq // BLOCK - 1, 0, -BLOCK)
        mask_prev = diff >= lo
        mask_next = diff <= hi
        vw = vcat[j * BLOCK:j * BLOCK + nkeys]
        ps, rs = [], []
        for hh in range(GQA_GROUP):
            sk = sink_ref[g * GQA_GROUP + hh] * LOG2E
            sh = s[hh * BLOCK:(hh + 1) * BLOCK]
            sh = jnp.concatenate([jnp.where(mask_prev, sh[:, :BLOCK], -jnp.inf), sh[:, BLOCK:2 * BLOCK],
                                  jnp.where(mask_next, sh[:, 2 * BLOCK:], -jnp.inf)], axis=1)
            m = jnp.maximum(jnp.max(sh, axis=-1, keepdims=True), sk)
            p = jnp.exp2(sh - m)
            denom = jnp.sum(p, axis=-1, keepdims=True) + jnp.exp2(sk - m)
            ps.append(p.astype(BF16))
            rs.append(1.0 / denom)
        p = jnp.concatenate(ps, axis=0)
        o = jnp.dot(p, vw, preferred_element_type=F32)
        return [o[hh * BLOCK:(hh + 1) * BLOCK] * rs[hh] for hh in range(GQA_GROUP)]

    pending = [scores(*u) for u in units[:ATTN_LOOKAHEAD]]
    outs = []
    for n, (j, g) in enumerate(units):
        if n + ATTN_LOOKAHEAD < len(units):
            pending.append(scores(*units[n + ATTN_LOOKAHEAD]))
        outs.append(softmax_pv(j, g, pending.pop(0)))
        if g == N_KV_HEADS - 1:
            a = jnp.concatenate([jnp.where(low, outs[0][hh], outs[1][hh]) for hh in range(GQA_GROUP)], axis=1)
            o_ref[0, j * BLOCK:(j + 1) * BLOCK, :] = _rms(a, gain, ATTN_WIDTH).astype(BF16)
            outs = []


def _attention(sink, q, k, v, out_gain):
    nb, s, _ = q.shape
    t = ATTN_TILE
    r = t // BLOCK
    last = s // BLOCK - 1
    main = lambda w: pl.BlockSpec((1, t, w), lambda b, i, *_: (b, i, 0))
    prev = pl.BlockSpec((1, BLOCK, KV_WIDTH), lambda b, i, *_: (b, jnp.maximum(i * r - 1, 0), 0))
    nxt = pl.BlockSpec((1, BLOCK, KV_WIDTH), lambda b, i, *_: (b, jnp.minimum((i + 1) * r, last), 0))
    grid_spec = pltpu.PrefetchScalarGridSpec(
        num_scalar_prefetch=1,
        grid=(nb, s // t),
        in_specs=[main(ATTN_WIDTH), prev, main(KV_WIDTH), nxt, prev, main(KV_WIDTH), nxt,
                  pl.BlockSpec((1, ATTN_WIDTH), lambda b, i, *_: (0, 0))],
        out_specs=main(ATTN_WIDTH),
    )
    return pl.pallas_call(
        functools.partial(_attn_kernel, seq=s),
        grid_spec=grid_spec,
        out_shape=jax.ShapeDtypeStruct((nb, s, ATTN_WIDTH), BF16),
        compiler_params=_COMPILER_PARAMS,
        name="attention",
    )(sink, q, k, k, k, v, v, v, out_gain)


def _dft1_kernel(m1_ref, cs_ref, u_ref, y_ref):
    m1 = m1_ref[...]
    cs = cs_ref[...]

    def channel_dft(j):
        slabs = []
        for t in range(F_SLABS):
            rows = u_ref.at[0, t].reshape(DFT_N * S1_GROUP, LANES)
            slabs.append(rows[pl.ds(j, DFT_N, stride=S1_GROUP), :].astype(BF16))
        per = MXU_DIM // LANES
        halves = [jnp.dot(jnp.concatenate(slabs[c * per:(c + 1) * per], axis=1), cs, preferred_element_type=F32)
                  for c in range(F_SLABS // per)]
        a = jnp.concatenate([h[:, :MXU_DIM] for h in halves], axis=1)
        b = jnp.concatenate([h[:, MXU_DIM:] for h in halves], axis=1)
        return jnp.concatenate([a, b], axis=0).astype(BF16)

    ab = [channel_dft(j) for j in range(S1_GROUP)]
    for j in range(S1_GROUP):
        y = jnp.dot(m1, ab[j], preferred_element_type=F32)
        y_ref[0, :, j * FOURIER_WIDTH:(j + 1) * FOURIER_WIDTH] = y.astype(BF16)


def _dft_stage1(m1, cs, u):
    nb = u.shape[0]
    groups = DFT_N // S1_GROUP
    u = u.reshape(nb, F_SLABS, DFT_N, groups, S1_GROUP, LANES)
    cols = S1_GROUP * FOURIER_WIDTH
    return pl.pallas_call(
        _dft1_kernel,
        grid=(nb, groups),
        in_specs=[_const_spec((2 * DFT_N, 2 * DFT_N)), _const_spec((MXU_DIM, 2 * MXU_DIM)),
                  pl.BlockSpec((1, F_SLABS, DFT_N, 1, S1_GROUP, LANES), lambda n, g: (n, 0, 0, g, 0, 0))],
        out_specs=pl.BlockSpec((1, 2 * DFT_N, cols), lambda n, g: (n, 0, g)),
        out_shape=jax.ShapeDtypeStruct((nb, 2 * DFT_N, DFT_N * FOURIER_WIDTH), BF16),
        compiler_params=_COMPILER_PARAMS,
        name="dft_stage1",
    )(m1, cs, u)


def _dft2_kernel(cn_ref, sn_ref, wc_ref, ws_ref, y_ref, wf_ref, g_ref, o_ref):
    cn = cn_ref[...]
    sn = sn_ref[...]
    wf = wf_ref[...]
    gain = g_ref[...]

    def seq_dft(j):
        wc = wc_ref[j:j + 1, :]
        ws = ws_ref[j:j + 1, :]
        gc = cn * wc - sn * ws
        gs = sn * wc + cn * ws
        g = jnp.concatenate([gc, gs], axis=1).astype(BF16)
        y = jnp.concatenate([y_ref[0, 0, j], y_ref[0, 1, j]], axis=0)
        return jnp.dot(g, y, preferred_element_type=F32).astype(BF16)

    xr = [seq_dft(j) for j in range(S2_K1)]
    fs = [jnp.dot(xr[j], wf, preferred_element_type=F32) for j in range(S2_K1)]
    for j in range(S2_K1):
        fn = _rms(fs[j], gain, FOURIER_WIDTH)
        for t in range(F_SLABS):
            rows = o_ref.at[0, t].reshape(DFT_N * S2_K1, LANES)
            rows[pl.ds(j, DFT_N, stride=S2_K1), :] = fn[:, t * LANES:(t + 1) * LANES]


def _dft_stage2(cn, sn, wc, ws, y, wf_bd, out_gain):
    nb = y.shape[0]
    y = y.reshape(nb, 2, DFT_N, DFT_N, FOURIER_WIDTH)
    groups = DFT_N // S2_K1
    tw = pl.BlockSpec((S2_K1, DFT_N), lambda n, k: (k, 0))
    out = pl.pallas_call(
        _dft2_kernel,
        grid=(nb, DFT_N // S2_K1),
        in_specs=[_const_spec((DFT_N, DFT_N)), _const_spec((DFT_N, DFT_N)), tw, tw,
                  pl.BlockSpec((1, 2, S2_K1, DFT_N, FOURIER_WIDTH), lambda n, k: (n, 0, k, 0, 0)),
                  _const_spec((FOURIER_WIDTH, FOURIER_WIDTH)), _const_spec((1, FOURIER_WIDTH))],
        out_specs=pl.BlockSpec((1, F_SLABS, DFT_N, 1, S2_K1, LANES), lambda n, k: (n, 0, 0, k, 0, 0)),
        out_shape=jax.ShapeDtypeStruct((nb, F_SLABS, DFT_N, groups, S2_K1, LANES), F32),
        compiler_params=_COMPILER_PARAMS,
        name="dft_stage2",
    )(cn, sn, wc, ws, y, wf_bd, out_gain)
    return out.reshape(nb, F_SLABS, DFT_N * DFT_N, LANES)


def _ffn_kernel(xm_ref, xp_ref, xn_ref, am_ref, ap_ref, an_ref, fm_ref, fp_ref, fn_ref, p_ref,
                wout_ref, gffn_ref, wup_ref, cw_ref, cb_ref, wdown_ref, wple_ref, gple_ref, wgate_ref,
                bgate_ref, o_ref, hu_scr, nat_scr):
    i = pl.program_id(1)
    nt = pl.num_programs(1)
    t = FFN_TILE
    half = t // 2
    ext = t + 2 * FFN_HALO
    trim = FFN_HALO - UP_HALO
    up_ext = t + 2 * UP_HALO
    x_all = jnp.concatenate([xp_ref[0], xm_ref[0], xn_ref[0]], axis=0)
    a_all = jnp.concatenate([ap_ref[0], am_ref[0], an_ref[0]], axis=0)
    f_all = [jnp.concatenate([fp_ref[0, s], fm_ref[0, s], fn_ref[0, s]], axis=0).astype(BF16)
             for s in range(F_SLABS)]
    mixed_all = jnp.concatenate([a_all] + f_all, axis=1)
    row = lax.broadcasted_iota(jnp.int32, (ext, 1), 0)

    def prologue(s):
        r0 = s * t
        x1 = x_all[r0:r0 + ext] + jnp.dot(mixed_all[r0:r0 + ext], wout_ref[...], preferred_element_type=F32)
        e = jnp.dot(p_ref[0, r0:r0 + t].astype(BF16), wple_ref[...], preferred_element_type=F32)
        e = _rms(e, gple_ref[...], D_MODEL)
        h = _rms(x1, gffn_ref[...], D_MODEL)
        first_valid = jnp.where(i == 0, FFN_HALO, 0) if s == 0 else 0
        end_valid = jnp.where(i == nt - 1, t + FFN_HALO, ext) if s == FFN_SUBTILES - 1 else ext
        h = jnp.where((row >= first_valid) & (row < end_valid), h, 0.0)
        return x1[FFN_HALO:FFN_HALO + t], e, h[trim:trim + up_ext].astype(BF16)

    def up_proj(h, j, slot):
        for part, c0 in enumerate((j * FFN_CHUNK, D_FF + j * FFN_CHUNK)):
            hu = jnp.dot(h, wup_ref[:, c0:c0 + FFN_CHUNK], preferred_element_type=F32)
            for c in range(FFN_CHUNK // LANES):
                hu_scr[slot, part, c] = hu[:, c * LANES:(c + 1) * LANES]

    def conv(slot, part, c0):
        cols = []
        for c in range(FFN_CHUNK // LANES):
            cw = cw_ref[:, c0 + c * LANES:c0 + (c + 1) * LANES]
            cb = cb_ref[:, c0 + c * LANES:c0 + (c + 1) * LANES]
            rows = [hu_scr[slot, part, c, pl.ds(UP_HALO - 1 + k, half, stride=2), :] for k in range(4)]
            even = rows[0] * cw[0:1] + rows[1] * cw[1:2] + rows[2] * cw[2:3] + cb
            odd = rows[1] * cw[0:1] + rows[2] * cw[1:2] + rows[3] * cw[2:3] + cb
            cols.append(jnp.concatenate([even, odd], axis=0))
        return jnp.concatenate(cols, axis=1)

    def epilogue(s, x1, e, acc):
        for c in range(D_MODEL // LANES):
            nat_scr[c, pl.ds(0, half, stride=2), :] = acc[:half, c * LANES:(c + 1) * LANES]
            nat_scr[c, pl.ds(1, half, stride=2), :] = acc[half:, c * LANES:(c + 1) * LANES]
        x2 = x1 + jnp.concatenate([nat_scr[c] for c in range(D_MODEL // LANES)], axis=1)
        gate = jax.nn.sigmoid(jnp.dot(x2.astype(BF16), wgate_ref[...], preferred_element_type=F32)
                              + bgate_ref[...])
        o_ref[0, s * t:(s + 1) * t, :] = x2 + gate * e

    units = [(s, j) for s in range(FFN_SUBTILES) for j in range(N_FFN_CHUNKS)]
    state = {}

    def issue(n):
        s, j = units[n]
        if j == 0:
            state[s] = list(prologue(s)) + [jnp.zeros((t, D_MODEL), F32)]
        up_proj(state[s][2], j, n % HU_SLOTS)

    for n in range(FFN_LOOKAHEAD):
        issue(n)
    for n, (s, j) in enumerate(units):
        if n + FFN_LOOKAHEAD < len(units):
            issue(n + FFN_LOOKAHEAD)
        gate = conv(n % HU_SLOTS, 0, j * FFN_CHUNK)
        up = conv(n % HU_SLOTS, 1, D_FF + j * FFN_CHUNK)
        act = (jax.nn.silu(gate) * up).astype(BF16)
        state[s][3] = state[s][3] + jnp.dot(act, wdown_ref[j * FFN_CHUNK:(j + 1) * FFN_CHUNK, :],
                                            preferred_element_type=F32)
        if j == N_FFN_CHUNKS - 1:
            x1, e, _, acc = state.pop(s)
            epilogue(s, x1, e, acc)


def _ffn(x, a, f, p, w_out, ffn_gain, w_up, conv_w, conv_b, w_down, w_ple, ple_gain, w_gate, b_gate):
    nb, s, _ = x.shape
    t = FFN_SUBTILES * FFN_TILE
    r = t // FFN_HALO
    last = s // FFN_HALO - 1

    def specs(w):
        return [pl.BlockSpec((1, t, w), lambda b, i: (b, i, 0)),
                pl.BlockSpec((1, FFN_HALO, w), lambda b, i: (b, jnp.maximum(i * r - 1, 0), 0)),
                pl.BlockSpec((1, FFN_HALO, w), lambda b, i: (b, jnp.minimum((i + 1) * r, last), 0))]

    f_specs = [pl.BlockSpec((1, F_SLABS, t, LANES), lambda b, i: (b, 0, i, 0)),
               pl.BlockSpec((1, F_SLABS, FFN_HALO, LANES), lambda b, i: (b, 0, jnp.maximum(i * r - 1, 0), 0)),
               pl.BlockSpec((1, F_SLABS, FFN_HALO, LANES), lambda b, i: (b, 0, jnp.minimum((i + 1) * r, last), 0))]

    in_specs = (specs(D_MODEL) + specs(ATTN_WIDTH) + f_specs
                + [pl.BlockSpec((1, t, PLE_DIM), lambda b, i: (b, i, 0)),
                   _const_spec((D_MODEL, D_MODEL)), _const_spec((1, D_MODEL)),
                   _const_spec((D_MODEL, 2 * D_FF)), _const_spec((3, 2 * D_FF)), _const_spec((1, 2 * D_FF)),
                   _const_spec((D_FF, D_MODEL)), _const_spec((PLE_DIM, D_MODEL)), _const_spec((1, D_MODEL)),
                   _const_spec((D_MODEL, D_MODEL)), _const_spec((1, D_MODEL))])
    return pl.pallas_call(
        _ffn_kernel,
        grid=(nb, s // t),
        in_specs=in_specs,
        out_specs=pl.BlockSpec((1, t, D_MODEL), lambda b, i: (b, i, 0)),
        out_shape=jax.ShapeDtypeStruct((nb, s, D_MODEL), F32),
        scratch_shapes=[pltpu.VMEM((HU_SLOTS, 2, FFN_CHUNK // LANES, FFN_TILE + 2 * UP_HALO, LANES), F32),
                        pltpu.VMEM((D_MODEL // LANES, FFN_TILE, LANES), F32)],
        compiler_params=_COMPILER_PARAMS,
        name="ffn",
    )(x, x, x, a, a, a, f, f, f, p, w_out, ffn_gain, w_up, conv_w, conv_b, w_down, w_ple, ple_gain,
      w_gate, b_gate)


def _dft_tables():
    n = np.arange(DFT_N)
    ang = 2.0 * np.pi * ((n[:, None] * n[None, :]) % DFT_N) / DFT_N
    cn, sn = np.cos(ang), np.sin(ang)
    m1 = np.block([[cn, -sn], [-sn, -cn]])
    seq = DFT_N * DFT_N
    tw = 2.0 * np.pi * ((n[:, None] * n[None, :]) % seq) / seq
    ortho = 1.0 / np.sqrt(seq * FOURIER_GROUP_DIM)
    c = np.arange(FOURIER_GROUP_DIM)
    ang_c = 2.0 * np.pi * ((c[:, None] * c[None, :]) % FOURIER_GROUP_DIM) / FOURIER_GROUP_DIM
    eye = np.eye(MXU_DIM // FOURIER_GROUP_DIM)
    cs = np.concatenate([np.kron(eye, np.cos(ang_c)), np.kron(eye, np.sin(ang_c))], axis=1)
    hsum = np.kron(np.eye(MXU_DIM // HEAD_DIM), np.ones((HEAD_DIM, HEAD_DIM)))
    f32 = lambda a: jnp.asarray(a, dtype=F32)
    return (f32(m1), f32(cn * ortho), f32(sn * ortho), f32(np.cos(tw)), f32(np.sin(tw)), f32(cs), f32(hsum))


def _rope_tables(seq):
    inv_freq = ROPE_THETA ** (-jnp.arange(0, ROT_DIM, 2, dtype=F32) / ROT_DIM)
    inv_head = jnp.concatenate([inv_freq, inv_freq, jnp.zeros((HEAD_DIM - ROT_DIM,), F32)])
    inv_lane = jnp.concatenate([inv_head] * (LANES // HEAD_DIM))[None, :]
    within = jnp.arange(IN_TILE, dtype=F32)[:, None] * inv_lane
    per_step = (jnp.arange(seq // IN_TILE, dtype=F32) * IN_TILE)[:, None] * inv_lane
    return jnp.cos(within), jnp.sin(within), jnp.cos(per_step), jnp.sin(per_step)


_HEAD_ORDER = tuple(g * GQA_GROUP + i for i in range(GQA_GROUP) for g in range(N_KV_HEADS))


def _permute_heads(a, axis):
    blocks = [lax.slice_in_dim(a, h * HEAD_DIM, (h + 1) * HEAD_DIM, axis=axis) for h in _HEAD_ORDER]
    return jnp.concatenate(blocks, axis=axis)


def _block_diag(w):
    g, c, e = w.shape
    eye = jnp.eye(g, dtype=w.dtype)
    return (eye[:, None, :, None] * w[:, :, None, :]).reshape(g * c, g * e)


def _layer(xs, ps, prm, tables):
    (attn_norm, w_in, q_norm, k_norm, sink, w_fourier, attn_out_norm, fourier_out_norm, w_out, ffn_norm,
     w_up, conv_w, conv_b, w_down, w_ple, ple_norm, w_ple_gate, b_ple_gate) = prm
    m1, cn, sn, wc, ws, cs_bd, hsum = tables
    row = lambda v: v.reshape(1, -1)
    two = lambda v: jnp.concatenate([v] * (LANES // HEAD_DIM)).reshape(1, LANES)

    w_in_p = jnp.concatenate([_permute_heads(w_in[:, :ATTN_WIDTH], 1), w_in[:, ATTN_WIDTH:]], axis=1)
    w_out_p = jnp.concatenate([_permute_heads(w_out[:ATTN_WIDTH], 0), w_out[ATTN_WIDTH:]], axis=0)
    proj = []
    for x in xs:
        assert x.shape[1] == DFT_N * DFT_N
        proj.append(_inproj(x, row(attn_norm), w_in_p.astype(BF16), two(q_norm), two(k_norm),
                            _rope_tables(x.shape[1]),
                            hsum.astype(BF16)))
    ys = [_dft_stage1(m1.astype(BF16), cs_bd.astype(BF16), u) for _, _, _, u in proj]
    attns = [_attention(sink, q, k, v, row(_permute_heads(attn_out_norm, 0))) for q, k, v, _ in proj]
    fours = [_dft_stage2(cn, sn, wc, ws, y, _block_diag(w_fourier).astype(BF16), row(fourier_out_norm))
             for y in ys]
    return [_ffn(x, attn, four, p, w_out_p.astype(BF16), row(ffn_norm), w_up.astype(BF16), conv_w,
                 row(conv_b), w_down.astype(BF16), w_ple.astype(BF16), row(ple_norm),
                 w_ple_gate.astype(BF16), row(b_ple_gate))
            for x, attn, four, p in zip(xs, attns, fours, ps)]


def kernel(x_prompt, x_sample, p_prompt, p_sample, attn_norm, w_in, q_norm, k_norm, sink, w_fourier,
           attn_out_norm, fourier_out_norm, w_out, ffn_norm, w_up, conv_w, conv_b, w_down, w_ple, ple_norm,
           w_ple_gate, b_ple_gate):
    stacked = (attn_norm, w_in, q_norm, k_norm, sink, w_fourier, attn_out_norm, fourier_out_norm, w_out,
               ffn_norm, w_up, conv_w, conv_b, w_down, w_ple, ple_norm, w_ple_gate, b_ple_gate)
    tables = _dft_tables()
    y_prompt, y_sample = x_prompt, x_sample
    for i in range(attn_norm.shape[0]):
        prm = tuple(w[i] for w in stacked)
        y_prompt, y_sample = _layer([y_prompt, y_sample], [p_prompt[i], p_sample[i]], prm, tables)
    return (y_prompt, y_sample)
```

```python
import functools

import numpy as np
import jax
import jax.numpy as jnp
from jax import lax
from jax.experimental import pallas as pl
from jax.experimental.pallas import tpu as pltpu

F32 = jnp.float32
BF16 = jnp.bfloat16

D_MODEL = 1024
HEAD_DIM = 64
N_HEADS = 8
N_KV_HEADS = 2
GQA_GROUP = N_HEADS // N_KV_HEADS
ATTN_WIDTH = N_HEADS * HEAD_DIM
KV_WIDTH = N_KV_HEADS * HEAD_DIM
FOURIER_WIDTH = 512
FOURIER_GROUP_DIM = 64
N_FOURIER_GROUPS = FOURIER_WIDTH // FOURIER_GROUP_DIM
IN_WIDTH = ATTN_WIDTH + 2 * KV_WIDTH + FOURIER_WIDTH
BLOCK = 128
ROPE_THETA = 500000.0
ROT_DIM = HEAD_DIM // 4
ROT_HALF = ROT_DIM // 2
D_FF = 2816
PLE_DIM = 256
EPS = 1e-6
LOG2E = 1.4426950408889634

LANES = 128
MXU_DIM = 256
BF16_ROWS = 16
VMEM_LIMIT_BYTES = 56 * 1024 * 1024

DFT_N = 128

IN_TILE = 2048
IN_SUB = 128
ATTN_TILE = 1024
ATTN_LOOKAHEAD = 2
FFN_LOOKAHEAD = 3
F32_ROWS = 8
S1_GROUP = F32_ROWS
F_SLABS = FOURIER_WIDTH // LANES
S2_K1 = F32_ROWS
FFN_TILE = 512
FFN_SUBTILES = 1
FFN_HALO = BF16_ROWS
UP_HALO = F32_ROWS
FFN_CHUNK = 256
N_FFN_CHUNKS = D_FF // FFN_CHUNK
HU_SLOTS = FFN_LOOKAHEAD + 1


_COMPILER_PARAMS = pltpu.CompilerParams(
    dimension_semantics=("parallel", "parallel"),
    vmem_limit_bytes=VMEM_LIMIT_BYTES,
)


def _rms(x, gain, n):
    ms = jnp.sum(x * x, axis=-1, keepdims=True) * (1.0 / n)
    return x * lax.rsqrt(ms + EPS) * gain


def _const_spec(shape):
    zeros = (0,) * len(shape)
    return pl.BlockSpec(shape, lambda *_: zeros, pipeline_mode=pl.Buffered(1))


def _inproj_kernel(x_ref, g_ref, win_ref, qg_ref, kg_ref, cosr_ref, sinr_ref, coss_ref, sins_ref, hsum_ref,
                   q_ref, k_ref, v_ref, u_ref):
    step = pl.program_id(1)
    cos_a = coss_ref[pl.ds(step, 1), :]
    sin_a = sins_ref[pl.ds(step, 1), :]
    d = lax.broadcasted_iota(jnp.int32, (IN_SUB, LANES), 1) % HEAD_DIM
    scale = HEAD_DIM ** -0.5 * LOG2E

    def project(r0):
        h = _rms(x_ref[0, r0:r0 + IN_SUB], g_ref[...], D_MODEL).astype(BF16)
        return jnp.dot(h, win_ref[...], preferred_element_type=F32)

    def finish(r0, z):
        cos_b = cosr_ref[r0:r0 + IN_SUB]
        sin_b = sinr_ref[r0:r0 + IN_SUB]
        cos_t = cos_a * cos_b - sin_a * sin_b
        sin_t = sin_a * cos_b + cos_a * sin_b
        sin_lo = jnp.where(d < ROT_HALF, -sin_t, 0.0)
        sin_hi = jnp.where(d >= ROT_HALF, sin_t, 0.0)

        def norm_rope(t, gain):
            w = t.shape[1]
            wide = lambda a: jnp.concatenate([a] * (w // LANES), axis=1)
            ssq = jnp.dot((t * t).astype(BF16), hsum_ref[:w, :w], preferred_element_type=F32)
            tn = t * lax.rsqrt(ssq * (1.0 / HEAD_DIM) + EPS) * wide(gain)
            up = pltpu.roll(tn, w - ROT_HALF, 1)
            dn = pltpu.roll(tn, ROT_HALF, 1)
            return tn * wide(cos_t) + up * wide(sin_lo) + dn * wide(sin_hi)

        for t in range(ATTN_WIDTH // MXU_DIM):
            qt = norm_rope(z[:, t * MXU_DIM:(t + 1) * MXU_DIM], qg_ref[...])
            q_ref[0, r0:r0 + IN_SUB, t * MXU_DIM:(t + 1) * MXU_DIM] = (qt * scale).astype(BF16)
        k_ref[0, r0:r0 + IN_SUB] = norm_rope(z[:, ATTN_WIDTH:ATTN_WIDTH + KV_WIDTH], kg_ref[...]).astype(BF16)
        v_ref[0, r0:r0 + IN_SUB] = z[:, ATTN_WIDTH + KV_WIDTH:ATTN_WIDTH + 2 * KV_WIDTH].astype(BF16)
        for t in range(F_SLABS):
            lo = ATTN_WIDTH + 2 * KV_WIDTH + t * LANES
            u_ref[0, t, r0:r0 + IN_SUB] = z[:, lo:lo + LANES]

    starts = list(range(0, IN_TILE, IN_SUB))
    z = project(starts[0])
    for n, r0 in enumerate(starts):
        z_next = project(starts[n + 1]) if n + 1 < len(starts) else None
        finish(r0, z)
        z = z_next


def _inproj(x, attn_norm, w_in, q_gain, k_gain, rope, hsum):
    nb, s, _ = x.shape
    t = IN_TILE
    tok = lambda w: pl.BlockSpec((1, t, w), lambda b, i: (b, i, 0))
    within = _const_spec((t, LANES))
    per_step = _const_spec((s // t, LANES))
    out_shape = (
        jax.ShapeDtypeStruct((nb, s, ATTN_WIDTH), BF16),
        jax.ShapeDtypeStruct((nb, s, KV_WIDTH), BF16),
        jax.ShapeDtypeStruct((nb, s, KV_WIDTH), BF16),
        jax.ShapeDtypeStruct((nb, F_SLABS, s, LANES), F32),
    )
    return pl.pallas_call(
        _inproj_kernel,
        grid=(nb, s // t),
        in_specs=[tok(D_MODEL), _const_spec((1, D_MODEL)), _const_spec((D_MODEL, IN_WIDTH)),
                  _const_spec((1, LANES)), _const_spec((1, LANES)), within, within, per_step, per_step,
                  _const_spec((MXU_DIM, MXU_DIM))],
        out_specs=(tok(ATTN_WIDTH), tok(KV_WIDTH), tok(KV_WIDTH),
                   pl.BlockSpec((1, F_SLABS, t, LANES), lambda b, i: (b, 0, i, 0))),
        out_shape=out_shape,
        compiler_params=_COMPILER_PARAMS,
        name="inproj",
    )(x, attn_norm, w_in, q_gain, k_gain, *rope, hsum)


def _attn_kernel(sink_ref, q_ref, kp_ref, km_ref, kn_ref, vp_ref, vm_ref, vn_ref, g_ref, o_ref, *, seq):
    i = pl.program_id(1)
    kcat = jnp.concatenate([kp_ref[0], km_ref[0], kn_ref[0]], axis=0)
    vcat = jnp.concatenate([vp_ref[0], vm_ref[0], vn_ref[0]], axis=0)
    nkeys = 3 * BLOCK
    diff = (lax.broadcasted_iota(jnp.int32, (BLOCK, BLOCK), 1)
            - lax.broadcasted_iota(jnp.int32, (BLOCK, BLOCK), 0))
    lane = lax.broadcasted_iota(jnp.int32, (BLOCK, LANES), 1)
    low = lane < HEAD_DIM
    gain = g_ref[...]
    units = [(j, g) for j in range(ATTN_TILE // BLOCK) for g in range(N_KV_HEADS)]

    def scores(j, g):
        kw = kcat[j * BLOCK:j * BLOCK + nkeys]
        sel = low if g == 0 else jnp.logical_not(low)
        qs = []
        for hh in range(GQA_GROUP):
            qt = q_ref[0, j * BLOCK:(j + 1) * BLOCK, hh * LANES:(hh + 1) * LANES]
            qs.append(jnp.where(sel, qt, jnp.zeros_like(qt)))
        qg = jnp.concatenate(qs, axis=0)
        return lax.dot_general(qg, kw, (((1,), (1,)), ((), ())), preferred_element_type=F32)

    def softmax_pv(j, g, s):
        qblk = i * (ATTN_TILE // BLOCK) + j
        lo = jnp.where(qblk > 0, 0, BLOCK)
        hi = jnp.where(qblk < seq // BLOCK - 1, 0, -BLOCK)
        mask_prev = diff >= lo
        mask_next = diff <= hi
        vw = vcat[j * BLOCK:j * BLOCK + nkeys]
        ps, rs = [], []
        for hh in range(GQA_GROUP):
            sk = sink_ref[g * GQA_GROUP + hh] * LOG2E
            sh = s[hh * BLOCK:(hh + 1) * BLOCK]
            sh = jnp.concatenate([jnp.where(mask_prev, sh[:, :BLOCK], -jnp.inf), sh[:, BLOCK:2 * BLOCK],
                                  jnp.where(mask_next, sh[:, 2 * BLOCK:], -jnp.inf)], axis=1)
            m = jnp.maximum(jnp.max(sh, axis=-1, keepdims=True), sk)
            p = jnp.exp2(sh - m)
            denom = jnp.sum(p, axis=-1, keepdims=True) + jnp.exp2(sk - m)
            ps.append(p.astype(BF16))
            rs.append(1.0 / denom)
        p = jnp.concatenate(ps, axis=0)
        o = jnp.dot(p, vw, preferred_element_type=F32)
        return [o[hh * BLOCK:(hh + 1) * BLOCK] * rs[hh] for hh in range(GQA_GROUP)]

    pending = [scores(*u) for u in units[:ATTN_LOOKAHEAD]]
    outs = []
    for n, (j, g) in enumerate(units):
        if n + ATTN_LOOKAHEAD < len(units):
            pending.append(scores(*units[n + ATTN_LOOKAHEAD]))
        outs.append(softmax_pv(j, g, pending.pop(0)))
        if g == N_KV_HEADS - 1:
            a = jnp.concatenate([jnp.where(low, outs[0][hh], outs[1][hh]) for hh in range(GQA_GROUP)], axis=1)
            o_ref[0, j * BLOCK:(j + 1) * BLOCK, :] = _rms(a, gain, ATTN_WIDTH).astype(BF16)
            outs = []


def _attention(sink, q, k, v, out_gain):
    nb, s, _ = q.shape
    t = ATTN_TILE
    r = t // BLOCK
    last = s // BLOCK - 1
    main = lambda w: pl.BlockSpec((1, t, w), lambda b, i, *_: (b, i, 0))
    prev = pl.BlockSpec((1, BLOCK, KV_WIDTH), lambda b, i, *_: (b, jnp.maximum(i * r - 1, 0), 0))
    nxt = pl.BlockSpec((1, BLOCK, KV_WIDTH), lambda b, i, *_: (b, jnp.minimum((i + 1) * r, last), 0))
    grid_spec = pltpu.PrefetchScalarGridSpec(
        num_scalar_prefetch=1,
        grid=(nb, s // t),
        in_specs=[main(ATTN_WIDTH), prev, main(KV_WIDTH), nxt, prev, main(KV_WIDTH), nxt,
                  pl.BlockSpec((1, ATTN_WIDTH), lambda b, i, *_: (0, 0))],
        out_specs=main(ATTN_WIDTH),
    )
    return pl.pallas_call(
        functools.partial(_attn_kernel, seq=s),
        grid_spec=grid_spec,
        out_shape=jax.ShapeDtypeStruct((nb, s, ATTN_WIDTH), BF16),
        compiler_params=_COMPILER_PARAMS,
        name="attention",
    )(sink, q, k, k, k, v, v, v, out_gain)


def _dft1_kernel(m1_ref, cs_ref, u_ref, after_ref, y_ref):
    m1 = m1_ref[...]
    cs = cs_ref[...]

    def channel_dft(j):
        slabs = []
        for t in range(F_SLABS):
            rows = u_ref.at[0, t].reshape(DFT_N * S1_GROUP, LANES)
            slabs.append(rows[pl.ds(j, DFT_N, stride=S1_GROUP), :].astype(BF16))
        per = MXU_DIM // LANES
        halves = [jnp.dot(jnp.concatenate(slabs[c * per:(c + 1) * per], axis=1), cs, preferred_element_type=F32)
                  for c in range(F_SLABS // per)]
        a = jnp.concatenate([h[:, :MXU_DIM] for h in halves], axis=1)
        b = jnp.concatenate([h[:, MXU_DIM:] for h in halves], axis=1)
        return jnp.concatenate([a, b], axis=0).astype(BF16)

    ab = [channel_dft(j) for j in range(S1_GROUP)]
    for j in range(S1_GROUP):
        y = jnp.dot(m1, ab[j], preferred_element_type=F32)
        y_ref[0, :, j * FOURIER_WIDTH:(j + 1) * FOURIER_WIDTH] = y.astype(BF16)


def _dft_stage1(m1, cs, u, after):
    nb = u.shape[0]
    groups = DFT_N // S1_GROUP
    u = u.reshape(nb, F_SLABS, DFT_N, groups, S1_GROUP, LANES)
    cols = S1_GROUP * FOURIER_WIDTH
    return pl.pallas_call(
        _dft1_kernel,
        grid=(nb, groups),
        in_specs=[_const_spec((2 * DFT_N, 2 * DFT_N)), _const_spec((MXU_DIM, 2 * MXU_DIM)),
                  pl.BlockSpec((1, F_SLABS, DFT_N, 1, S1_GROUP, LANES), lambda n, g: (n, 0, 0, g, 0, 0)),
                  _const_spec((1, BF16_ROWS, KV_WIDTH))],
        out_specs=pl.BlockSpec((1, 2 * DFT_N, cols), lambda n, g: (n, 0, g)),
        out_shape=jax.ShapeDtypeStruct((nb, 2 * DFT_N, DFT_N * FOURIER_WIDTH), BF16),
        compiler_params=_COMPILER_PARAMS,
        name="dft_stage1",
    )(m1, cs, u, after)


def _dft2_kernel(cn_ref, sn_ref, wc_ref, ws_ref, y_ref, wf_ref, g_ref, o_ref):
    cn = cn_ref[...]
    sn = sn_ref[...]
    wf = wf_ref[...]
    gain = g_ref[...]

    def seq_dft(j):
        wc = wc_ref[j:j + 1, :]
        ws = ws_ref[j:j + 1, :]
        gc = cn * wc - sn * ws
        gs = sn * wc + cn * ws
        g = jnp.concatenate([gc, gs], axis=1).astype(BF16)
        y = jnp.concatenate([y_ref[0, 0, j], y_ref[0, 1, j]], axis=0)
        return jnp.dot(g, y, preferred_element_type=F32).astype(BF16)

    xr = [seq_dft(j) for j in range(S2_K1)]
    fs = [jnp.dot(xr[j], wf, preferred_element_type=F32) for j in range(S2_K1)]
    for j in range(S2_K1):
        fn = _rms(fs[j], gain, FOURIER_WIDTH)
        for t in range(F_SLABS):
            rows = o_ref.at[0, t].reshape(DFT_N * S2_K1, LANES)
            rows[pl.ds(j, DFT_N, stride=S2_K1), :] = fn[:, t * LANES:(t + 1) * LANES]


def _dft_stage2(cn, sn, wc, ws, y, wf_bd, out_gain):
    nb = y.shape[0]
    y = y.reshape(nb, 2, DFT_N, DFT_N, FOURIER_WIDTH)
    groups = DFT_N // S2_K1
    tw = pl.BlockSpec((S2_K1, DFT_N), lambda n, k: (k, 0))
    out = pl.pallas_call(
        _dft2_kernel,
        grid=(nb, DFT_N // S2_K1),
        in_specs=[_const_spec((DFT_N, DFT_N)), _const_spec((DFT_N, DFT_N)), tw, tw,
                  pl.BlockSpec((1, 2, S2_K1, DFT_N, FOURIER_WIDTH), lambda n, k: (n, 0, k, 0, 0)),
                  _const_spec((FOURIER_WIDTH, FOURIER_WIDTH)), _const_spec((1, FOURIER_WIDTH))],
        out_specs=pl.BlockSpec((1, F_SLABS, DFT_N, 1, S2_K1, LANES), lambda n, k: (n, 0, 0, k, 0, 0)),
        out_shape=jax.ShapeDtypeStruct((nb, F_SLABS, DFT_N, groups, S2_K1, LANES), F32),
        compiler_params=_COMPILER_PARAMS,
        name="dft_stage2",
    )(cn, sn, wc, ws, y, wf_bd, out_gain)
    return out.reshape(nb, F_SLABS, DFT_N * DFT_N, LANES)


def _ffn_kernel(xm_ref, xp_ref, xn_ref, am_ref, ap_ref, an_ref, fm_ref, fp_ref, fn_ref, p_ref,
                wout_ref, gffn_ref, wup_ref, cw_ref, cb_ref, wdown_ref, wple_ref, gple_ref, wgate_ref,
                bgate_ref, o_ref, hu_scr, nat_scr):
    i = pl.program_id(1)
    nt = pl.num_programs(1)
    t = FFN_TILE
    half = t // 2
    ext = t + 2 * FFN_HALO
    trim = FFN_HALO - UP_HALO
    up_ext = t + 2 * UP_HALO
    x_all = jnp.concatenate([xp_ref[0], xm_ref[0], xn_ref[0]], axis=0)
    a_all = jnp.concatenate([ap_ref[0], am_ref[0], an_ref[0]], axis=0)
    f_all = [jnp.concatenate([fp_ref[0, s], fm_ref[0, s], fn_ref[0, s]], axis=0).astype(BF16)
             for s in range(F_SLABS)]
    mixed_all = jnp.concatenate([a_all] + f_all, axis=1)
    row = lax.broadcasted_iota(jnp.int32, (ext, 1), 0)

    def prologue(s):
        r0 = s * t
        x1 = x_all[r0:r0 + ext] + jnp.dot(mixed_all[r0:r0 + ext], wout_ref[...], preferred_element_type=F32)
        e = jnp.dot(p_ref[0, r0:r0 + t].astype(BF16), wple_ref[...], preferred_element_type=F32)
        e = _rms(e, gple_ref[...], D_MODEL)
        h = _rms(x1, gffn_ref[...], D_MODEL)
        first_valid = jnp.where(i == 0, FFN_HALO, 0) if s == 0 else 0
        end_valid = jnp.where(i == nt - 1, t + FFN_HALO, ext) if s == FFN_SUBTILES - 1 else ext
        h = jnp.where((row >= first_valid) & (row < end_valid), h, 0.0)
        return x1[FFN_HALO:FFN_HALO + t], e, h[trim:trim + up_ext].astype(BF16)

    def up_proj(h, j, slot):
        for part, c0 in enumerate((j * FFN_CHUNK, D_FF + j * FFN_CHUNK)):
            hu = jnp.dot(h, wup_ref[:, c0:c0 + FFN_CHUNK], preferred_element_type=F32)
            for c in range(FFN_CHUNK // LANES):
                hu_scr[slot, part, c] = hu[:, c * LANES:(c + 1) * LANES]

    def conv(slot, part, c0):
        cols = []
        for c in range(FFN_CHUNK // LANES):
            cw = cw_ref[:, c0 + c * LANES:c0 + (c + 1) * LANES]
            cb = cb_ref[:, c0 + c * LANES:c0 + (c + 1) * LANES]
            rows = [hu_scr[slot, part, c, pl.ds(UP_HALO - 1 + k, half, stride=2), :] for k in range(4)]
            even = rows[0] * cw[0:1] + rows[1] * cw[1:2] + rows[2] * cw[2:3] + cb
            odd = rows[1] * cw[0:1] + rows[2] * cw[1:2] + rows[3] * cw[2:3] + cb
            cols.append(jnp.concatenate([even, odd], axis=0))
        return jnp.concatenate(cols, axis=1)

    def epilogue(s, x1, e, acc):
        for c in range(D_MODEL // LANES):
            nat_scr[c, pl.ds(0, half, stride=2), :] = acc[:half, c * LANES:(c + 1) * LANES]
            nat_scr[c, pl.ds(1, half, stride=2), :] = acc[half:, c * LANES:(c + 1) * LANES]
        x2 = x1 + jnp.concatenate([nat_scr[c] for c in range(D_MODEL // LANES)], axis=1)
        gate = jax.nn.sigmoid(jnp.dot(x2.astype(BF16), wgate_ref[...], preferred_element_type=F32)
                              + bgate_ref[...])
        o_ref[0, s * t:(s + 1) * t, :] = x2 + gate * e

    units = [(s, j) for s in range(FFN_SUBTILES) for j in range(N_FFN_CHUNKS)]
    state = {}

    def issue(n):
        s, j = units[n]
        if j == 0:
            state[s] = list(prologue(s)) + [jnp.zeros((t, D_MODEL), F32)]
        up_proj(state[s][2], j, n % HU_SLOTS)

    for n in range(FFN_LOOKAHEAD):
        issue(n)
    for n, (s, j) in enumerate(units):
        if n + FFN_LOOKAHEAD < len(units):
            issue(n + FFN_LOOKAHEAD)
        gate = conv(n % HU_SLOTS, 0, j * FFN_CHUNK)
        up = conv(n % HU_SLOTS, 1, D_FF + j * FFN_CHUNK)
        act = (jax.nn.silu(gate) * up).astype(BF16)
        state[s][3] = state[s][3] + jnp.dot(act, wdown_ref[j * FFN_CHUNK:(j + 1) * FFN_CHUNK, :],
                                            preferred_element_type=F32)
        if j == N_FFN_CHUNKS - 1:
            x1, e, _, acc = state.pop(s)
            epilogue(s, x1, e, acc)


def _ffn(x, a, f, p, w_out, ffn_gain, w_up, conv_w, conv_b, w_down, w_ple, ple_gain, w_gate, b_gate):
    nb, s, _ = x.shape
    t = FFN_SUBTILES * FFN_TILE
    r = t // FFN_HALO
    last = s // FFN_HALO - 1

    def specs(w):
        return [pl.BlockSpec((1, t, w), lambda b, i: (b, i, 0)),
                pl.BlockSpec((1, FFN_HALO, w), lambda b, i: (b, jnp.maximum(i * r - 1, 0), 0)),
                pl.BlockSpec((1, FFN_HALO, w), lambda b, i: (b, jnp.minimum((i + 1) * r, last), 0))]

    f_specs = [pl.BlockSpec((1, F_SLABS, t, LANES), lambda b, i: (b, 0, i, 0)),
               pl.BlockSpec((1, F_SLABS, FFN_HALO, LANES), lambda b, i: (b, 0, jnp.maximum(i * r - 1, 0), 0)),
               pl.BlockSpec((1, F_SLABS, FFN_HALO, LANES), lambda b, i: (b, 0, jnp.minimum((i + 1) * r, last), 0))]

    in_specs = (specs(D_MODEL) + specs(ATTN_WIDTH) + f_specs
                + [pl.BlockSpec((1, t, PLE_DIM), lambda b, i: (b, i, 0)),
                   _const_spec((D_MODEL, D_MODEL)), _const_spec((1, D_MODEL)),
                   _const_spec((D_MODEL, 2 * D_FF)), _const_spec((3, 2 * D_FF)), _const_spec((1, 2 * D_FF)),
                   _const_spec((D_FF, D_MODEL)), _const_spec((PLE_DIM, D_MODEL)), _const_spec((1, D_MODEL)),
                   _const_spec((D_MODEL, D_MODEL)), _const_spec((1, D_MODEL))])
    return pl.pallas_call(
        _ffn_kernel,
        grid=(nb, s // t),
        in_specs=in_specs,
        out_specs=pl.BlockSpec((1, t, D_MODEL), lambda b, i: (b, i, 0)),
        out_shape=jax.ShapeDtypeStruct((nb, s, D_MODEL), F32),
        scratch_shapes=[pltpu.VMEM((HU_SLOTS, 2, FFN_CHUNK // LANES, FFN_TILE + 2 * UP_HALO, LANES), F32),
                        pltpu.VMEM((D_MODEL // LANES, FFN_TILE, LANES), F32)],
        compiler_params=_COMPILER_PARAMS,
        name="ffn",
    )(x, x, x, a, a, a, f, f, f, p, w_out, ffn_gain, w_up, conv_w, conv_b, w_down, w_ple, ple_gain,
      w_gate, b_gate)


def _dft_tables():
    n = np.arange(DFT_N)
    ang = 2.0 * np.pi * ((n[:, None] * n[None, :]) % DFT_N) / DFT_N
    cn, sn = np.cos(ang), np.sin(ang)
    m1 = np.block([[cn, -sn], [-sn, -cn]])
    seq = DFT_N * DFT_N
    tw = 2.0 * np.pi * ((n[:, None] * n[None, :]) % seq) / seq
    ortho = 1.0 / np.sqrt(seq * FOURIER_GROUP_DIM)
    c = np.arange(FOURIER_GROUP_DIM)
    ang_c = 2.0 * np.pi * ((c[:, None] * c[None, :]) % FOURIER_GROUP_DIM) / FOURIER_GROUP_DIM
    eye = np.eye(MXU_DIM // FOURIER_GROUP_DIM)
    cs = np.concatenate([np.kron(eye, np.cos(ang_c)), np.kron(eye, np.sin(ang_c))], axis=1)
    hsum = np.kron(np.eye(MXU_DIM // HEAD_DIM), np.ones((HEAD_DIM, HEAD_DIM)))
    f32 = lambda a: jnp.asarray(a, dtype=F32)
    return (f32(m1), f32(cn * ortho), f32(sn * ortho), f32(np.cos(tw)), f32(np.sin(tw)), f32(cs), f32(hsum))


def _rope_tables(seq):
    inv_freq = ROPE_THETA ** (-jnp.arange(0, ROT_DIM, 2, dtype=F32) / ROT_DIM)
    inv_head = jnp.concatenate([inv_freq, inv_freq, jnp.zeros((HEAD_DIM - ROT_DIM,), F32)])
    inv_lane = jnp.concatenate([inv_head] * (LANES // HEAD_DIM))[None, :]
    within = jnp.arange(IN_TILE, dtype=F32)[:, None] * inv_lane
    per_step = (jnp.arange(seq // IN_TILE, dtype=F32) * IN_TILE)[:, None] * inv_lane
    return jnp.cos(within), jnp.sin(within), jnp.cos(per_step), jnp.sin(per_step)


_HEAD_ORDER = tuple(g * GQA_GROUP + i for i in range(GQA_GROUP) for g in range(N_KV_HEADS))


def _permute_heads(a, axis):
    blocks = [lax.slice_in_dim(a, h * HEAD_DIM, (h + 1) * HEAD_DIM, axis=axis) for h in _HEAD_ORDER]
    return jnp.concatenate(blocks, axis=axis)


def _block_diag(w):
    g, c, e = w.shape
    eye = jnp.eye(g, dtype=w.dtype)
    return (eye[:, None, :, None] * w[:, :, None, :]).reshape(g * c, g * e)


def _layer(xs, ps, prm, tables):
    (attn_norm, w_in, q_norm, k_norm, sink, w_fourier, attn_out_norm, fourier_out_norm, w_out, ffn_norm,
     w_up, conv_w, conv_b, w_down, w_ple, ple_norm, w_ple_gate, b_ple_gate) = prm
    m1, cn, sn, wc, ws, cs_bd, hsum = tables
    row = lambda v: v.reshape(1, -1)
    two = lambda v: jnp.concatenate([v] * (LANES // HEAD_DIM)).reshape(1, LANES)

    w_in_p = jnp.concatenate([_permute_heads(w_in[:, :ATTN_WIDTH], 1), w_in[:, ATTN_WIDTH:]], axis=1)
    w_out_p = jnp.concatenate([_permute_heads(w_out[:ATTN_WIDTH], 0), w_out[ATTN_WIDTH:]], axis=0)
    proj = []
    for x in xs:
        assert x.shape[1] == DFT_N * DFT_N
        proj.append(_inproj(x, row(attn_norm), w_in_p.astype(BF16), two(q_norm), two(k_norm),
                            _rope_tables(x.shape[1]),
                            hsum.astype(BF16)))
    last_k = proj[-1][1][:1, :BF16_ROWS]
    ys = [_dft_stage1(m1.astype(BF16), cs_bd.astype(BF16), u, last_k) for _, _, _, u in proj]
    attns = [_attention(sink, q, k, v, row(_permute_heads(attn_out_norm, 0))) for q, k, v, _ in proj]
    fours = [_dft_stage2(cn, sn, wc, ws, y, _block_diag(w_fourier).astype(BF16), row(fourier_out_norm))
             for y in ys]
    return [_ffn(x, attn, four, p, w_out_p.astype(BF16), row(ffn_norm), w_up.astype(BF16), conv_w,
                 row(conv_b), w_down.astype(BF16), w_ple.astype(BF16), row(ple_norm),
                 w_ple_gate.astype(BF16), row(b_ple_gate))
            for x, attn, four, p in zip(xs, attns, fours, ps)]


def kernel(x_prompt, x_sample, p_prompt, p_sample, attn_norm, w_in, q_norm, k_norm, sink, w_fourier,
           attn_out_norm, fourier_out_norm, w_out, ffn_norm, w_up, conv_w, conv_b, w_down, w_ple, ple_norm,
           w_ple_gate, b_ple_gate):
    stacked = (attn_norm, w_in, q_norm, k_norm, sink, w_fourier, attn_out_norm, fourier_out_norm, w_out,
               ffn_norm, w_up, conv_w, conv_b, w_down, w_ple, ple_norm, w_ple_gate, b_ple_gate)
    tables = _dft_tables()
    y_prompt, y_sample = x_prompt, x_sample
    for i in range(attn_norm.shape[0]):
        prm = tuple(w[i] for w in stacked)
        y_prompt, y_sample = _layer([y_prompt, y_sample], [p_prompt[i], p_sample[i]], prm, tables)
    return (y_prompt, y_sample)
```
